```python
import math
import jax, jax.numpy as jnp
from jax import lax
import numpy as np

D_MODEL = 1024
BATCH = 16
SEQ = 2048
DEPTH = 1
DEC_BATCH = 128
DEC_SEQ = 8
PAST_LEN = 8192
PAGE_SIZE = 128

ATT_WIDTH = D_MODEL // 2
HEAD_DIM = 64
N_HEADS = ATT_WIDTH // HEAD_DIM
DILATED_BRANCHES = ((128, 1), (512, 4), (2048, 16))
MAX_WINDOW = max(w for w, _ in DILATED_BRANCHES)
NUM_BUCKETS = 32
MAX_DISTANCE = MAX_WINDOW
POOL_WIDTH = D_MODEL - ATT_WIDTH
POOL_WINDOWS = (2, 4, 8, 16)
N_POOL_GROUPS = len(POOL_WINDOWS)
POOL_GROUP = POOL_WIDTH // N_POOL_GROUPS
POOL_CTX = max(POOL_WINDOWS) - 1
D_FF = 4 * D_MODEL
N_ADA = 6
EPS = 1e-6
NEG_INF = -1e30

kernel_name = "hymba_dilated_pool_adaln_decoder_step"


def rms_norm(x, g):
    xf = x.astype(jnp.float32)
    y = xf * lax.rsqrt(jnp.mean(xf * xf, axis=-1, keepdims=True) + EPS)
    return (y * g.astype(jnp.float32)).astype(x.dtype)


def rel_bucket(dist):
    exact = NUM_BUCKETS // 2
    d = jnp.maximum(dist.astype(jnp.float32), 1.0)
    large = exact + (jnp.log(d / exact) / math.log(MAX_DISTANCE / exact)
                     * (NUM_BUCKETS - exact)).astype(jnp.int32)
    large = jnp.minimum(large, NUM_BUCKETS - 1)
    return jnp.where(dist < exact, dist, large)


def _adaln(c, w_ada, b_ada):
    a = jax.nn.silu(c) @ w_ada + b_ada
    return jnp.split(a[:, None, :], N_ADA, axis=-1)


def _mixer_in(x, shift, scale, norm_g, w_in, q_g, k_g):
    B, T, _ = x.shape
    h = rms_norm(x, norm_g) * (1 + scale) + shift
    z = h @ w_in
    q = z[..., :ATT_WIDTH].reshape(B, T, N_HEADS, HEAD_DIM)
    k = z[..., ATT_WIDTH:2 * ATT_WIDTH].reshape(B, T, N_HEADS, HEAD_DIM)
    v = z[..., 2 * ATT_WIDTH:3 * ATT_WIDTH].reshape(B, T, N_HEADS, HEAD_DIM)
    u = z[..., 3 * ATT_WIDTH:]
    q = rms_norm(q, q_g) * (HEAD_DIM ** -0.5)
    k = rms_norm(k, k_g)
    return q, k, v, u


def _dilated_prompt(q, k, v, window, dil, rel_bias):
    B, S, H, E = q.shape
    n_sub = window // dil
    L = S // dil
    blk = min(n_sub, L)
    nb = -(-L // blk)
    Lp = nb * blk

    def split(a):
        a = a.reshape(B, L, dil, H, E)
        a = jnp.pad(a, ((0, 0), (0, Lp - L), (0, 0), (0, 0), (0, 0)))
        return a.reshape(B, nb, blk, dil, H, E)

    def with_prev(a):
        prev = jnp.concatenate([jnp.zeros_like(a[:, :1]), a[:, :-1]], axis=1)
        return jnp.concatenate([prev, a], axis=2)

    qb = split(q)
    kk = with_prev(split(k))
    vv = with_prev(split(v))
    logits = jnp.einsum('bnqrhe,bnkrhe->bnrhqk', qb, kk).astype(jnp.float32)
    qi = jnp.arange(blk)[:, None]
    ki = jnp.arange(2 * blk)[None, :]
    dist = qi + blk - ki
    key_sub = jnp.arange(nb)[:, None, None] * blk - blk + ki[None]
    valid = (dist >= 0)[None] & (dist <= n_sub)[None] & (key_sub >= 0)
    bias = rel_bias[rel_bucket(jnp.clip(dist, 0, n_sub) * dil)]
    logits = logits + jnp.transpose(bias, (2, 0, 1)).astype(jnp.float32)
    logits = jnp.where(valid[None, :, None, None], logits, NEG_INF)
    lse = jax.nn.logsumexp(logits, axis=-1)
    p = jnp.exp(logits - lse[..., None])
    o = jnp.einsum('bnrhqk,bnkrhe->bnqrhe', p, vv.astype(jnp.float32))
    o = o.reshape(B, Lp, dil, H, E)[:, :L].reshape(B, S, H, E)
    lse = jnp.transpose(lse, (0, 1, 4, 2, 3)).reshape(B, Lp, dil, H)[:, :L].reshape(B, S, H)
    return o, lse


def _dilated_sample(q, k_all, v_all, window, dil, rel_bias):
    B, T, H, E = q.shape
    n_past = k_all.shape[1] - T
    n_sub = window // dil
    j = jnp.arange(n_sub + 1)
    idx = n_past + jnp.arange(T)[:, None] - dil * j[None, :]
    valid = idx >= 0
    idx = jnp.maximum(idx, 0)
    kg = k_all[:, idx]
    vg = v_all[:, idx]
    logits = jnp.einsum('bthe,btjhe->bthj', q, kg).astype(jnp.float32)
    bias = rel_bias[rel_bucket(j * dil)]
    logits = logits + jnp.transpose(bias).astype(jnp.float32)[None, None]
    logits = jnp.where(valid[None, :, None, :], logits, NEG_INF)
    lse = jax.nn.logsumexp(logits, axis=-1)
    p = jnp.exp(logits - lse[..., None])
    o = jnp.einsum('bthj,btjhe->bthe', p, vg.astype(jnp.float32))
    return o, lse


def _merge_branches(results):
    outs = jnp.stack([o for o, _ in results])
    lses = jnp.stack([s for _, s in results])
    wts = jax.nn.softmax(lses, axis=0)
    o = jnp.sum(wts[..., None] * outs, axis=0)
    return o.reshape(o.shape[0], o.shape[1], ATT_WIDTH)


def _pool_mix(u_ext, pos0, w_pool, pool_scale):
    B, n, C = u_ext.shape
    T = n - POOL_CTX
    uf = u_ext.astype(jnp.float32)
    cs = jnp.pad(jnp.cumsum(uf, axis=1), ((0, 0), (1, 0), (0, 0)))
    end = POOL_CTX + 1 + jnp.arange(T)
    pos = (pos0 + jnp.arange(T)).astype(jnp.float32)
    cur = uf[:, POOL_CTX:]
    groups = []
    for g, w in enumerate(POOL_WINDOWS):
        lo, hi = g * POOL_GROUP, (g + 1) * POOL_GROUP
        win_sum = cs[:, end, lo:hi] - cs[:, end - w, lo:hi]
        cnt = jnp.minimum(float(w), pos + 1.0)[None, :, None]
        groups.append(win_sum / cnt - cur[:, :, lo:hi])
    pooled = jnp.stack(groups, axis=2)
    y = jnp.einsum('btgc,gcd->btgd', pooled, w_pool.astype(jnp.float32)).reshape(B, T, C)
    return y * pool_scale


def _layer_tail(x, attn, pool, gate1, shift2, scale2, gate2, norm2_g, w_out, w_up, w_down):
    mix = jnp.concatenate([attn, pool], axis=-1).astype(x.dtype) @ w_out
    x = x + gate1 * mix
    h = rms_norm(x, norm2_g) * (1 + scale2) + shift2
    f = jnp.square(jax.nn.relu(h @ w_up)) @ w_down
    return x + gate2 * f


def setup_inputs(seed: int = 0) -> dict:
    key = jax.random.key(seed)
    ks = jax.random.split(key, 20)
    nrm = jax.random.normal
    wb = min(MAX_WINDOW, PAST_LEN)
    d_in = 3 * ATT_WIDTH + POOL_WIDTH
    return {
        "x_prompt": nrm(ks[0], (BATCH, SEQ, D_MODEL), jnp.float32),
        "x_sample": nrm(ks[1], (DEC_BATCH, DEC_SEQ, D_MODEL), jnp.float32),
        "c_prompt": nrm(ks[2], (BATCH, D_MODEL), jnp.float32),
        "c_sample": nrm(ks[3], (DEC_BATCH, D_MODEL), jnp.float32),
        "cache_k": nrm(ks[4], (DEPTH, DEC_BATCH, wb, N_HEADS, HEAD_DIM), jnp.float32),
        "cache_v": nrm(ks[5], (DEPTH, DEC_BATCH, wb, N_HEADS, HEAD_DIM), jnp.float32),
        "state_pool": nrm(ks[6], (DEPTH, DEC_BATCH, POOL_CTX, POOL_WIDTH), jnp.float32),
        "w_ada": nrm(ks[7], (DEPTH, D_MODEL, N_ADA * D_MODEL), jnp.float32) * (0.5 * D_MODEL ** -0.5),
        "b_ada": 0.01 * nrm(ks[8], (DEPTH, N_ADA * D_MODEL), jnp.float32),
        "norm1_g": 1.0 + 0.05 * nrm(ks[9], (DEPTH, D_MODEL), jnp.float32),
        "norm2_g": 1.0 + 0.05 * nrm(ks[10], (DEPTH, D_MODEL), jnp.float32),
        "w_in": nrm(ks[11], (DEPTH, D_MODEL, d_in), jnp.float32) * D_MODEL ** -0.5,
        "q_norm_g": 1.0 + 0.05 * nrm(ks[12], (DEPTH, HEAD_DIM), jnp.float32),
        "k_norm_g": 1.0 + 0.05 * nrm(ks[13], (DEPTH, HEAD_DIM), jnp.float32),
        "rel_bias": 0.5 * nrm(ks[14], (NUM_BUCKETS, N_HEADS), jnp.float32),
        "w_pool": nrm(ks[15], (DEPTH, N_POOL_GROUPS, POOL_GROUP, POOL_GROUP), jnp.float32) * POOL_GROUP ** -0.5,
        "pool_scale": 1.0 + 0.1 * nrm(ks[16], (DEPTH, POOL_WIDTH), jnp.float32),
        "w_out": nrm(ks[17], (DEPTH, D_MODEL, D_MODEL), jnp.float32) * D_MODEL ** -0.5,
        "w_up": nrm(ks[18], (DEPTH, D_MODEL, D_FF), jnp.float32) * D_MODEL ** -0.5,
        "w_down": nrm(ks[19], (DEPTH, D_FF, D_MODEL), jnp.float32) * D_FF ** -0.5,
    }


def reference(x_prompt, x_sample, c_prompt, c_sample, cache_k, cache_v, state_pool,
              w_ada, b_ada, norm1_g, norm2_g, w_in, q_norm_g, k_norm_g, rel_bias,
              w_pool, pool_scale, w_out, w_up, w_down):
    yp, ys = x_prompt, x_sample
    kp, vp, pp, ksm, vsm, psm = [], [], [], [], [], []
    for l in range(DEPTH):
        m = _adaln(c_prompt, w_ada[l], b_ada[l])
        q, k, v, u = _mixer_in(yp, m[0], m[1], norm1_g[l], w_in[l], q_norm_g[l], k_norm_g[l])
        attn = _merge_branches([_dilated_prompt(q, k, v, w, d, rel_bias) for w, d in DILATED_BRANCHES])
        u_ext = jnp.pad(u, ((0, 0), (POOL_CTX, 0), (0, 0)))
        pool = _pool_mix(u_ext, 0, w_pool[l], pool_scale[l])
        keep = min(MAX_WINDOW, k.shape[1])
        kp.append(k[:, -keep:])
        vp.append(v[:, -keep:])
        pp.append(u[:, -POOL_CTX:])
        yp = _layer_tail(yp, attn, pool, m[2], m[3], m[4], m[5],
                         norm2_g[l], w_out[l], w_up[l], w_down[l])

        m = _adaln(c_sample, w_ada[l], b_ada[l])
        q, k, v, u = _mixer_in(ys, m[0], m[1], norm1_g[l], w_in[l], q_norm_g[l], k_norm_g[l])
        k_all = jnp.concatenate([cache_k[l].astype(k.dtype), k], axis=1)
        v_all = jnp.concatenate([cache_v[l].astype(v.dtype), v], axis=1)
        attn = _merge_branches([_dilated_sample(q, k_all, v_all, w, d, rel_bias) for w, d in DILATED_BRANCHES])
        u_ext = jnp.concatenate([state_pool[l].astype(u.dtype), u], axis=1)
        pool = _pool_mix(u_ext, PAST_LEN, w_pool[l], pool_scale[l])
        wb = cache_k.shape[2]
        ksm.append(k_all[:, -wb:])
        vsm.append(v_all[:, -wb:])
        psm.append(u_ext[:, -POOL_CTX:])
        ys = _layer_tail(ys, attn, pool, m[2], m[3], m[4], m[5],
                         norm2_g[l], w_out[l], w_up[l], w_down[l])

    k_win_prompt = jnp.stack(kp)
    v_win_prompt = jnp.stack(vp)
    pool_prompt = jnp.stack(pp)
    k_win_sample = jnp.stack(ksm)
    v_win_sample = jnp.stack(vsm)
    pool_sample = jnp.stack(psm)
    return (yp, ys, k_win_prompt, v_win_prompt, pool_prompt, k_win_sample, v_win_sample, pool_sample)
```

```python
import functools
import math

import jax
import jax.numpy as jnp
from jax import lax
from jax.experimental import pallas as pl
from jax.experimental.pallas import tpu as pltpu

F32 = jnp.float32
BF16 = jnp.bfloat16

HEAD_DIM = 64
DILATED_BRANCHES = ((128, 1), (512, 4), (2048, 16))
NUM_BUCKETS = 32
MAX_DISTANCE = 2048
POOL_WINDOWS = (2, 4, 8, 16)
PAST_LEN = 8192
POOL_CTX = max(POOL_WINDOWS) - 1
POOL_PAD = 16
N_ADA = 6
EPS = 1e-6
NEG_INF = -1e30

LANES = 128
SEG_W = 256
VMEM_LIMIT = 56 * 1024 * 1024


def _rel_bucket(dist):
    exact = NUM_BUCKETS // 2
    d = jnp.maximum(dist.astype(F32), 1.0)
    large = exact + (jnp.log(d / exact) / math.log(MAX_DISTANCE / exact)
                     * (NUM_BUCKETS - exact)).astype(jnp.int32)
    large = jnp.minimum(large, NUM_BUCKETS - 1)
    return jnp.where(dist < exact, dist, large)


def _bias_from_buckets(bkt, relb_ref, head):
    out = jnp.full(bkt.shape, NEG_INF, F32)
    for b in range(NUM_BUCKETS):
        out = jnp.where(bkt == b, relb_ref[b, head], out)
    return out


def _adaln_kernel(c_ref, w_ref, b_ref, o_ref):
    c = c_ref[...]
    s = c / (1.0 + jnp.exp(-c))
    o_ref[...] = jnp.dot(s, w_ref[...], preferred_element_type=F32,
                         precision=lax.Precision.HIGHEST) + b_ref[...]


def _adaln(c, w, b, tn=1024):
    m, d = c.shape
    n = w.shape[1]
    return pl.pallas_call(
        _adaln_kernel,
        out_shape=jax.ShapeDtypeStruct((m, n), F32),
        grid=(n // tn,),
        in_specs=[pl.BlockSpec((m, d), lambda j: (0, 0)),
                  pl.BlockSpec((d, tn), lambda j: (0, j)),
                  pl.BlockSpec((1, tn), lambda j: (0, j))],
        out_specs=pl.BlockSpec((m, tn), lambda j: (0, j)),
        compiler_params=pltpu.CompilerParams(dimension_semantics=("arbitrary",),
                                             vmem_limit_bytes=VMEM_LIMIT),
        name="adaln",
    )(c, w, b.reshape(1, n))


def _modulated_norm(x, g, shift, scale, groups):
    tm, d = x.shape
    ms = jnp.mean(x * x, axis=-1, keepdims=True)
    y = x * lax.rsqrt(ms + EPS) * g
    if groups == 1:
        return y * (1.0 + scale) + shift
    y3 = y.reshape(groups, tm // groups, d)
    return (y3 * (1.0 + scale[:, None, :]) + shift[:, None, :]).reshape(tm, d)


def _per_group(v, x, groups):
    tm, d = x.shape
    if groups == 1:
        return v * x
    return (v[:, None, :] * x.reshape(groups, tm // groups, d)).reshape(tm, d)


def _mixer_in_kernel(x_ref, shift_ref, scale_ref, g1_ref, w_ref, qg_ref, kg_ref, seg_ref,
                     q_ref, k_ref, v_ref, u_ref, *, groups):
    a = q_ref.shape[1]
    h = _modulated_norm(x_ref[...], g1_ref[...], shift_ref[...], scale_ref[...], groups)
    hb = h.astype(BF16)

    def head_norm(z, g):
        parts = []
        for c in range(a // SEG_W):
            zc = z[:, c * SEG_W:(c + 1) * SEG_W]
            ms = jnp.dot((zc * zc).astype(BF16), seg_ref[...], preferred_element_type=F32)
            parts.append(zc * lax.rsqrt(ms + EPS))
        return jnp.concatenate(parts, axis=-1) * g

    zq = jnp.dot(hb, w_ref[:, 0:a], preferred_element_type=F32)
    q_ref[...] = head_norm(zq, qg_ref[...]) * (HEAD_DIM ** -0.5)
    zk = jnp.dot(hb, w_ref[:, a:2 * a], preferred_element_type=F32)
    k_ref[...] = head_norm(zk, kg_ref[...])
    v_ref[...] = jnp.dot(hb, w_ref[:, 2 * a:3 * a], preferred_element_type=F32)
    u_ref[...] = jnp.dot(hb, w_ref[:, 3 * a:], preferred_element_type=F32)


def _mixer_in(x2, shift, scale, g1, w_in_b, qg, kg, seg, *, tm, rows_per_mod):
    rows, d = x2.shape
    d_in = w_in_b.shape[1]
    a = qg.shape[1]
    p = d_in - 3 * a
    if rows_per_mod >= tm:
        groups = 1
        per = rows_per_mod // tm
        shift = shift.reshape(-1, 1, d)
        scale = scale.reshape(-1, 1, d)
        mod_spec = pl.BlockSpec((None, 1, d), lambda i: (i // per, 0, 0))
    else:
        groups = tm // rows_per_mod
        mod_spec = pl.BlockSpec((groups, d), lambda i: (i, 0))
    const = lambda i: (0, 0)
    row_spec = lambda w: pl.BlockSpec((tm, w), lambda i: (i, 0))
    return pl.pallas_call(
        functools.partial(_mixer_in_kernel, groups=groups),
        out_shape=[jax.ShapeDtypeStruct((rows, a), F32)] * 3 + [jax.ShapeDtypeStruct((rows, p), F32)],
        grid=(rows // tm,),
        in_specs=[row_spec(d), mod_spec, mod_spec,
                  pl.BlockSpec((1, d), const),
                  pl.BlockSpec((d, d_in), const),
                  pl.BlockSpec((1, a), const), pl.BlockSpec((1, a), const),
                  pl.BlockSpec((SEG_W, SEG_W), const)],
        out_specs=[row_spec(a)] * 3 + [row_spec(p)],
        compiler_params=pltpu.CompilerParams(dimension_semantics=("arbitrary",),
                                             vmem_limit_bytes=VMEM_LIMIT),
        name="mixer_in",
    )(x2, shift, scale, g1, w_in_b, qg, kg, seg)


def _prompt_attn_kernel(relb_ref, bkt_ref, q_ref, k_ref, v_ref, o_ref,
                        qd, kd, vd, bias_s, acc_s, m_s, l_s, *, blk):
    s_len = q_ref.shape[0]
    pair = pl.program_id(0)
    lane = lax.broadcasted_iota(jnp.int32, (1, LANES), 1)
    head_mask = [lane < HEAD_DIM, lane >= HEAD_DIM]

    @pl.when(pl.program_id(1) == 0)
    def _():
        for bi in range(len(DILATED_BRANCHES)):
            for hh in range(2):
                bias_s[bi, hh] = _bias_from_buckets(bkt_ref[bi], relb_ref, 2 * pair + hh)

    for bi, (_, dil) in enumerate(DILATED_BRANCHES):
        sub = s_len // dil
        for src, dst in ((q_ref, qd), (k_ref, kd), (v_ref, vd)):
            for r in range(dil):
                rows = src[...] if dil == 1 else src[pl.ds(r, sub, stride=dil), :]
                dst[bi, r * sub:(r + 1) * sub, :] = rows.astype(BF16)

    def unit(bi, dil, base, out_start, first):
        nk = blk if first else 2 * blk
        kbase = base if first else base - blk
        qb = qd[bi, pl.ds(base, blk), :]
        kk = kd[bi, pl.ds(kbase, nk), :]
        vv = vd[bi, pl.ds(kbase, nk), :]
        accs, ms, ls = [], [], []
        for hh in range(2):
            qh = jnp.where(head_mask[hh], qb, jnp.zeros_like(qb))
            s = lax.dot_general(qh, kk, (((1,), (1,)), ((), ())), preferred_element_type=F32)
            s = s + (bias_s[bi, hh, :, blk:] if first else bias_s[bi, hh])
            m = jnp.max(s, axis=-1, keepdims=True)
            p = jnp.exp(s - m)
            ls.append(jnp.sum(p, axis=-1, keepdims=True))
            ms.append(m)
            accs.append(jnp.dot(p.astype(BF16), vv, preferred_element_type=F32))
        full = (blk, LANES)
        acc = jnp.where(head_mask[0], accs[0], accs[1])
        mm = jnp.where(head_mask[0], jnp.broadcast_to(ms[0], full), jnp.broadcast_to(ms[1], full))
        ll = jnp.where(head_mask[0], jnp.broadcast_to(ls[0], full), jnp.broadcast_to(ls[1], full))
        if dil == 1:
            dst = pl.ds(base, blk)
        else:
            dst = pl.ds(out_start, blk, stride=dil)
        acc_s[bi, dst, :] = acc
        m_s[bi, dst, :] = mm
        l_s[bi, dst, :] = ll

    for bi, (_, dil) in enumerate(DILATED_BRANCHES):
        sub = s_len // dil
        nb = sub // blk

        def residue(r, carry, bi=bi, dil=dil, sub=sub, nb=nb):
            unit(bi, dil, pl.multiple_of(r * sub, blk), r, True)

            def later(n, c):
                unit(bi, dil, pl.multiple_of(r * sub + n * blk, blk), n * (blk * dil) + r, False)
                return c

            lax.fori_loop(1, nb, later, 0)
            return carry

        lax.fori_loop(0, dil, residue, 0)

    def merge(c, carry):
        rows = pl.ds(pl.multiple_of(c * blk, blk), blk)
        m = [m_s[bi, rows, :] for bi in range(3)]
        top = jnp.maximum(jnp.maximum(m[0], m[1]), m[2])
        num = jnp.zeros((blk, LANES), F32)
        den = jnp.zeros((blk, LANES), F32)
        for bi in range(3):
            w = jnp.exp(m[bi] - top)
            num = num + w * acc_s[bi, rows, :]
            den = den + w * l_s[bi, rows, :]
        o_ref[rows, :] = num / den
        return carry

    lax.fori_loop(0, s_len // blk, merge, 0)


def _prompt_attention(q, k, v, rel_bias, blk=128):
    b, s_len, a = q.shape
    n_pairs = a // LANES
    nbr = len(DILATED_BRANCHES)
    assert all(w // d == blk and (s_len // d) % blk == 0 for w, d in DILATED_BRANCHES)
    qi = jnp.arange(blk)[:, None]
    ki = jnp.arange(2 * blk)[None, :]
    dist = qi + blk - ki
    ok = (dist >= 0) & (dist <= blk)
    bkt = jnp.stack([jnp.where(ok, _rel_bucket(jnp.clip(dist, 0, blk) * d), -1)
                     for _, d in DILATED_BRANCHES]).astype(jnp.int32)
    seq_spec = pl.BlockSpec((None, s_len, LANES), lambda p, i: (i, 0, p))
    return pl.pallas_call(
        functools.partial(_prompt_attn_kernel, blk=blk),
        out_shape=jax.ShapeDtypeStruct((b, s_len, a), F32),
        grid=(n_pairs, b),
        in_specs=[pl.BlockSpec(memory_space=pltpu.SMEM),
                  pl.BlockSpec((nbr, blk, 2 * blk), lambda p, i: (0, 0, 0)),
                  seq_spec, seq_spec, seq_spec],
        out_specs=seq_spec,
        scratch_shapes=[pltpu.VMEM((nbr, s_len, LANES), BF16)] * 3
                       + [pltpu.VMEM((nbr, 2, blk, 2 * blk), F32)]
                       + [pltpu.VMEM((nbr, s_len, LANES), F32)] * 3,
        compiler_params=pltpu.CompilerParams(dimension_semantics=("arbitrary", "arbitrary"),
                                             vmem_limit_bytes=VMEM_LIMIT),
        name="prompt_attn",
    )(rel_bias, bkt, q, k, v)


def _sample_attn_kernel(relb_ref, bkt_ref, mult_ref, q_ref, kn_ref, vn_ref, ck_ref, cv_ref,
                        o_ref, ko_ref, vo_ref, kall, vall, bias_s, *, slab, row_chunk):
    t = q_ref.shape[0]
    a, wb = ck_ref.shape
    nk = kall.shape[1]
    pad = nk - wb
    n_slab = a // slab
    hps = slab // HEAD_DIM

    @pl.when(pl.program_id(0) == 0)
    def _():
        for h in range(a // HEAD_DIM):
            bias_s[h // hps, (h % hps) * t:(h % hps + 1) * t, :] = _bias_from_buckets(bkt_ref[...], relb_ref, h)

    def append(new_ref, cache_ref, out_ref, all_ref):
        new_t = jnp.concatenate([new_ref[...], jnp.zeros((pad - t, a), F32)], axis=0).T
        for c in range(a // row_chunk):
            rows = slice(c * row_chunk, (c + 1) * row_chunk)
            ext = jnp.concatenate([cache_ref[rows, :], new_t[rows, :]], axis=1)
            all_ref[rows, :] = ext.astype(BF16)
            out_ref[rows, :] = pltpu.roll(ext, nk - t, axis=1)[:, 0:wb]

    append(kn_ref, ck_ref, ko_ref, kall)
    append(vn_ref, cv_ref, vo_ref, vall)

    lane = lax.broadcasted_iota(jnp.int32, (hps * t, slab), 1)
    row = lax.broadcasted_iota(jnp.int32, (hps * t, slab), 0)
    own = (lane // HEAD_DIM) == (row // t)
    mult = jnp.concatenate([mult_ref[...]] * hps, axis=0)
    for sl in range(n_slab):
        cols = slice(sl * slab, (sl + 1) * slab)
        qs = jnp.concatenate([q_ref[:, cols]] * hps, axis=0)
        qs = jnp.where(own, qs, 0.0).astype(BF16)
        s = jnp.dot(qs, kall[cols, :], preferred_element_type=F32) + bias_s[sl]
        m = jnp.max(s, axis=-1, keepdims=True)
        p = jnp.exp(s - m) * mult
        l = jnp.sum(p, axis=-1, keepdims=True)
        o = lax.dot_general(p.astype(BF16), vall[cols, :], (((1,), (1,)), ((), ())),
                            preferred_element_type=F32) / l
        o = jnp.where(own, o, 0.0)
        res = o[0:t]
        for hh in range(1, hps):
            res = res + o[hh * t:(hh + 1) * t]
        o_ref[:, cols] = res


def _sample_attention(q, k_new, v_new, cache_kt, cache_vt, rel_bias, slab=256, pad=128, row_chunk=64):
    db, t, a = q.shape
    wb = cache_kt.shape[2]
    nk = wb + pad
    assert pad >= t and wb >= max(w for w, _ in DILATED_BRANCHES)
    dist = wb + jnp.arange(t)[:, None] - jnp.arange(nk)[None, :]
    mult = jnp.zeros((t, nk), jnp.int32)
    for w, d in DILATED_BRANCHES:
        mult = mult + ((dist >= 0) & (dist <= w) & (dist % d == 0)).astype(jnp.int32)
    bkt = jnp.where(mult > 0, _rel_bucket(jnp.maximum(dist, 0)), -1).astype(jnp.int32)
    hps = slab // HEAD_DIM
    new_spec = pl.BlockSpec((None, t, a), lambda i: (i, 0, 0))
    win_spec = pl.BlockSpec((None, a, wb), lambda i: (i, 0, 0))
    tab_spec = pl.BlockSpec((t, nk), lambda i: (0, 0))
    return pl.pallas_call(
        functools.partial(_sample_attn_kernel, slab=slab, row_chunk=row_chunk),
        out_shape=[jax.ShapeDtypeStruct((db, t, a), F32),
                   jax.ShapeDtypeStruct((db, a, wb), F32),
                   jax.ShapeDtypeStruct((db, a, wb), F32)],
        grid=(db,),
        in_specs=[pl.BlockSpec(memory_space=pltpu.SMEM), tab_spec, tab_spec,
                  new_spec, new_spec, new_spec, win_spec, win_spec],
        out_specs=[new_spec, win_spec, win_spec],
        scratch_shapes=[pltpu.VMEM((a, nk), BF16), pltpu.VMEM((a, nk), BF16),
                        pltpu.VMEM((a // slab, hps * t, nk), F32)],
        compiler_params=pltpu.CompilerParams(dimension_semantics=("arbitrary",),
                                             vmem_limit_bytes=VMEM_LIMIT),
        name="sample_attn",
    )(rel_bias, bkt, mult.astype(F32), q, k_new, v_new, cache_kt, cache_vt)


def _pool_kernel(u_ref, ctx_ref, wp_ref, ps_ref, o_ref, ext, *, carry, pos0):
    nb, t, p = u_ref.shape
    grp = p // len(POOL_WINDOWS)
    if carry:
        step = pl.program_id(1)

        @pl.when(step == 0)
        def _():
            ext[:, 0:POOL_PAD, :] = jnp.zeros((nb, POOL_PAD, p), F32)

        @pl.when(step > 0)
        def _():
            ext[:, 0:POOL_PAD, :] = ext[:, t:t + POOL_PAD, :]

        pos = (pos0 + step * t + lax.broadcasted_iota(jnp.int32, (1, t, 1), 1)).astype(F32)
    else:
        ext[:, POOL_PAD - POOL_CTX:POOL_PAD, :] = ctx_ref[...]
        pos = (pos0 + lax.broadcasted_iota(jnp.int32, (1, t, 1), 1)).astype(F32)
    ext[:, POOL_PAD:POOL_PAD + t, :] = u_ref[...]

    for g, w in enumerate(POOL_WINDOWS):
        cols = slice(g * grp, (g + 1) * grp)
        cur = ext[:, POOL_PAD:POOL_PAD + t, cols]
        win = cur
        for i in range(1, w):
            win = win + ext[:, POOL_PAD - i:POOL_PAD - i + t, cols]
        cnt = jnp.minimum(float(w), pos + 1.0)
        pooled = (win / cnt - cur).reshape(nb * t, grp)
        y = jnp.dot(pooled.astype(BF16), wp_ref[g], preferred_element_type=F32)
        o_ref[:, :, cols] = (y * ps_ref[:, cols]).reshape(nb, t, grp)


def _pool_mix(u, ctx, w_pool_b, pool_scale, *, tile, pos0):
    nb, t, p = u.shape
    g = len(POOL_WINDOWS)
    carry = ctx is None
    if carry:
        grid = (nb, t // tile)
        blk = (1, tile, p)
        u_spec = pl.BlockSpec(blk, lambda i, j: (i, j, 0))
        ctx = jnp.zeros((1, POOL_CTX, p), F32)
        ctx_spec = pl.BlockSpec((1, POOL_CTX, p), lambda i, j: (0, 0, 0))
        const3 = lambda i, j: (0, 0, 0)
        const2 = lambda i, j: (0, 0)
        sem = ("arbitrary", "arbitrary")
    else:
        assert tile == t
        grid = (1,)
        blk = (nb, t, p)
        u_spec = pl.BlockSpec(blk, lambda i: (0, 0, 0))
        ctx_spec = pl.BlockSpec((nb, POOL_CTX, p), lambda i: (0, 0, 0))
        const3 = lambda i: (0, 0, 0)
        const2 = lambda i: (0, 0)
        sem = ("arbitrary",)
    return pl.pallas_call(
        functools.partial(_pool_kernel, carry=carry, pos0=pos0),
        out_shape=jax.ShapeDtypeStruct((nb, t, p), F32),
        grid=grid,
        in_specs=[u_spec, ctx_spec,
                  pl.BlockSpec((g, p // g, p // g), const3),
                  pl.BlockSpec((1, p), const2)],
        out_specs=u_spec,
        scratch_shapes=[pltpu.VMEM((blk[0], POOL_PAD + tile, p), F32)],
        compiler_params=pltpu.CompilerParams(dimension_semantics=sem, vmem_limit_bytes=VMEM_LIMIT),
        name="pool_mix",
    )(u, ctx, w_pool_b, pool_scale)


def _tail_kernel(x_ref, attn_ref, pool_ref, gate1_ref, shift2_ref, scale2_ref, gate2_ref, g2_ref,
                 wout_ref, wup_ref, wdown_ref, y_ref, *, groups, ff_chunk):
    a = attn_ref.shape[1]
    d_ff = wup_ref.shape[1]
    mix = jnp.dot(attn_ref[...].astype(BF16), wout_ref[0:a, :], preferred_element_type=F32)
    mix = mix + jnp.dot(pool_ref[...].astype(BF16), wout_ref[a:, :], preferred_element_type=F32)
    x1 = x_ref[...] + _per_group(gate1_ref[...], mix, groups)
    hb = _modulated_norm(x1, g2_ref[...], shift2_ref[...], scale2_ref[...], groups).astype(BF16)
    f = jnp.zeros(x1.shape, F32)
    for c in range(d_ff // ff_chunk):
        cols = slice(c * ff_chunk, (c + 1) * ff_chunk)
        act = jnp.maximum(jnp.dot(hb, wup_ref[:, cols], preferred_element_type=F32), 0.0)
        f = f + jnp.dot((act * act).astype(BF16), wdown_ref[cols, :], preferred_element_type=F32)
    y_ref[...] = x1 + _per_group(gate2_ref[...], f, groups)


def _layer_tail(x2, attn2, pool2, gate1, shift2, scale2, gate2, g2, w_out_b, w_up_b, w_down_b,
                *, tm, rows_per_mod, ff_chunk=1024):
    rows, d = x2.shape
    a = attn2.shape[1]
    p = pool2.shape[1]
    d_ff = w_up_b.shape[1]
    mods = [gate1, shift2, scale2, gate2]
    if rows_per_mod >= tm:
        groups = 1
        per = rows_per_mod // tm
        mods = [m.reshape(-1, 1, d) for m in mods]
        mod_spec = pl.BlockSpec((None, 1, d), lambda i: (i // per, 0, 0))
    else:
        groups = tm // rows_per_mod
        mod_spec = pl.BlockSpec((groups, d), lambda i: (i, 0))
    const = lambda i: (0, 0)
    row_spec = lambda w: pl.BlockSpec((tm, w), lambda i: (i, 0))
    resident = lambda shape: pl.BlockSpec(shape, const, pipeline_mode=pl.Buffered(1))
    return pl.pallas_call(
        functools.partial(_tail_kernel, groups=groups, ff_chunk=ff_chunk),
        out_shape=jax.ShapeDtypeStruct((rows, d), F32),
        grid=(rows // tm,),
        in_specs=[row_spec(d), row_spec(a), row_spec(p), mod_spec, mod_spec, mod_spec, mod_spec,
                  pl.BlockSpec((1, d), const),
                  resident((a + p, d)), resident((d, d_ff)), resident((d_ff, d))],
        out_specs=row_spec(d),
        compiler_params=pltpu.CompilerParams(dimension_semantics=("arbitrary",),
                                             vmem_limit_bytes=VMEM_LIMIT),
        name="layer_tail",
    )(x2, attn2, pool2, *mods, g2, w_out_b, w_up_b, w_down_b)


def kernel(x_prompt, x_sample, c_prompt, c_sample, cache_k, cache_v, state_pool, w_ada, b_ada, norm1_g, norm2_g, w_in, q_norm_g, k_norm_g, rel_bias, w_pool, pool_scale, w_out, w_up, w_down):
    b, s_len, d = x_prompt.shape
    db, t, _ = x_sample.shape
    depth = w_ada.shape[0]
    assert depth == 1, "single-layer step"
    n_heads = rel_bias.shape[1]
    a = n_heads * HEAD_DIM
    wb = cache_k.shape[2]
    l = 0

    ada = _adaln(jnp.concatenate([c_prompt, c_sample], axis=0), w_ada[l], b_ada[l])
    mods = [ada[:, i * d:(i + 1) * d] for i in range(N_ADA)]
    mp = [m[:b] for m in mods]
    msm = [m[b:] for m in mods]

    w_in_b = w_in[l].astype(BF16)
    w_out_b = w_out[l].astype(BF16)
    w_up_b = w_up[l].astype(BF16)
    w_down_b = w_down[l].astype(BF16)
    w_pool_b = w_pool[l].astype(BF16)
    g1 = norm1_g[l].reshape(1, d)
    g2 = norm2_g[l].reshape(1, d)
    qg = jnp.tile(q_norm_g[l], n_heads).reshape(1, a)
    kg = jnp.tile(k_norm_g[l], n_heads).reshape(1, a)
    ps = pool_scale[l].reshape(1, -1)
    seg_i = jnp.arange(SEG_W) // HEAD_DIM
    seg = jnp.where(seg_i[:, None] == seg_i[None, :], 1.0 / HEAD_DIM, 0.0).astype(BF16)

    xp2 = x_prompt.reshape(b * s_len, d)
    q, k, v, u = _mixer_in(xp2, mp[0], mp[1], g1, w_in_b, qg, kg, seg, tm=512, rows_per_mod=s_len)
    attn = _prompt_attention(q.reshape(b, s_len, a), k.reshape(b, s_len, a), v.reshape(b, s_len, a), rel_bias)
    u3 = u.reshape(b, s_len, -1)
    pool = _pool_mix(u3, None, w_pool_b, ps, tile=256, pos0=0)
    y_prompt = _layer_tail(xp2, attn.reshape(b * s_len, a), pool.reshape(b * s_len, -1),
                           mp[2], mp[3], mp[4], mp[5], g2, w_out_b, w_up_b, w_down_b,
                           tm=512, rows_per_mod=s_len).reshape(b, s_len, d)
    keep = min(max(w for w, _ in DILATED_BRANCHES), s_len)
    k_win_prompt = k.reshape(b, s_len, n_heads, HEAD_DIM)[None, :, s_len - keep:]
    v_win_prompt = v.reshape(b, s_len, n_heads, HEAD_DIM)[None, :, s_len - keep:]
    pool_prompt = u3[None, :, s_len - POOL_CTX:]

    xs2 = x_sample.reshape(db * t, d)
    qs, ks, vs, us = _mixer_in(xs2, msm[0], msm[1], g1, w_in_b, qg, kg, seg, tm=256, rows_per_mod=t)
    to_fm = lambda c: jnp.transpose(c, (0, 2, 3, 1)).reshape(db, a, wb)
    from_fm = lambda c: jnp.transpose(c.reshape(db, n_heads, HEAD_DIM, wb), (0, 3, 1, 2))[None]
    attn_s, k_win, v_win = _sample_attention(
        qs.reshape(db, t, a), ks.reshape(db, t, a), vs.reshape(db, t, a),
        to_fm(cache_k[l]), to_fm(cache_v[l]), rel_bias)
    us3 = us.reshape(db, t, -1)
    pool_s = _pool_mix(us3, state_pool[l], w_pool_b, ps, tile=t, pos0=PAST_LEN)
    y_sample = _layer_tail(xs2, attn_s.reshape(db * t, a), pool_s.reshape(db * t, -1),
                           msm[2], msm[3], msm[4], msm[5], g2, w_out_b, w_up_b, w_down_b,
                           tm=256, rows_per_mod=t).reshape(db, t, d)
    k_win_sample = from_fm(k_win)
    v_win_sample = from_fm(v_win)
    pool_sample = jnp.concatenate([state_pool[l], us3], axis=1)[None, :, -POOL_CTX:]

    return (y_prompt, y_sample, k_win_prompt, v_win_prompt, pool_prompt,
            k_win_sample, v_win_sample, pool_sample)
```

```python
import functools
import math

import jax
import jax.numpy as jnp
from jax import lax
from jax.experimental import pallas as pl
from jax.experimental.pallas import tpu as pltpu

F32 = jnp.float32
BF16 = jnp.bfloat16

HEAD_DIM = 64
DILATED_BRANCHES = ((128, 1), (512, 4), (2048, 16))
NUM_BUCKETS = 32
MAX_DISTANCE = 2048
POOL_WINDOWS = (2, 4, 8, 16)
PAST_LEN = 8192
POOL_CTX = max(POOL_WINDOWS) - 1
POOL_PAD = 16
POOL_LEAD = max(POOL_WINDOWS) // 2
N_ADA = 6
EPS = 1e-6
NEG_INF = -1e30

LANES = 128
SEG_W = 256
VMEM_LIMIT = 56 * 1024 * 1024


def _rel_bucket(dist):
    exact = NUM_BUCKETS // 2
    d = jnp.maximum(dist.astype(F32), 1.0)
    large = exact + (jnp.log(d / exact) / math.log(MAX_DISTANCE / exact)
                     * (NUM_BUCKETS - exact)).astype(jnp.int32)
    large = jnp.minimum(large, NUM_BUCKETS - 1)
    return jnp.where(dist < exact, dist, large)


def _bias_from_buckets(bkt, relb_ref, head):
    out = jnp.full(bkt.shape, NEG_INF, F32)
    for b in range(NUM_BUCKETS):
        out = jnp.where(bkt == b, relb_ref[b, head], out)
    return out


def _adaln_kernel(c_ref, w_ref, b_ref, o_ref):
    c = c_ref[...]
    s = c / (1.0 + jnp.exp(-c))
    o_ref[...] = jnp.dot(s, w_ref[...], preferred_element_type=F32,
                         precision=lax.Precision.HIGHEST) + b_ref[...]


def _adaln(c, w, b, tn=1024):
    m, d = c.shape
    n = w.shape[1]
    return pl.pallas_call(
        _adaln_kernel,
        out_shape=jax.ShapeDtypeStruct((m, n), F32),
        grid=(n // tn,),
        in_specs=[pl.BlockSpec((m, d), lambda j: (0, 0)),
                  pl.BlockSpec((d, tn), lambda j: (0, j)),
                  pl.BlockSpec((1, tn), lambda j: (0, j))],
        out_specs=pl.BlockSpec((m, tn), lambda j: (0, j)),
        compiler_params=pltpu.CompilerParams(dimension_semantics=("arbitrary",),
                                             vmem_limit_bytes=VMEM_LIMIT),
        name="adaln",
    )(c, w, b.reshape(1, n))


def _modulated_norm(x, g, shift, scale, groups):
    tm, d = x.shape
    ms = jnp.mean(x * x, axis=-1, keepdims=True)
    y = x * lax.rsqrt(ms + EPS) * g
    if groups == 1:
        return y * (1.0 + scale) + shift
    y3 = y.reshape(groups, tm // groups, d)
    return (y3 * (1.0 + scale[:, None, :]) + shift[:, None, :]).reshape(tm, d)


def _per_group(v, x, groups):
    tm, d = x.shape
    if groups == 1:
        return v * x
    return (v[:, None, :] * x.reshape(groups, tm // groups, d)).reshape(tm, d)


def _mixer_in_kernel(x_ref, shift_ref, scale_ref, g1_ref, w_ref, qg_ref, kg_ref, seg_ref,
                     q_ref, k_ref, v_ref, u_ref, *fm_refs, groups):
    a = q_ref.shape[1]
    h = _modulated_norm(x_ref[...], g1_ref[...], shift_ref[...], scale_ref[...], groups)
    hb = h.astype(BF16)

    def head_norm(z, g):
        parts = []
        for c in range(a // SEG_W):
            zc = z[:, c * SEG_W:(c + 1) * SEG_W]
            ms = jnp.dot((zc * zc).astype(BF16), seg_ref[...], preferred_element_type=F32)
            parts.append(zc * lax.rsqrt(ms + EPS))
        return jnp.concatenate(parts, axis=-1) * g

    zq = jnp.dot(hb, w_ref[:, 0:a], preferred_element_type=F32)
    q_ref[...] = head_norm(zq, qg_ref[...]) * (HEAD_DIM ** -0.5)
    zk = jnp.dot(hb, w_ref[:, a:2 * a], preferred_element_type=F32)
    k = head_norm(zk, kg_ref[...])
    k_ref[...] = k
    v = jnp.dot(hb, w_ref[:, 2 * a:3 * a], preferred_element_type=F32)
    v_ref[...] = v
    u_ref[...] = jnp.dot(hb, w_ref[:, 3 * a:], preferred_element_type=F32)
    if fm_refs:
        kt_ref, vt_ref = fm_refs
        kt_ref[...] = k.T
        vt_ref[...] = v.T


def _mixer_in(x2, shift, scale, g1, w_in_b, qg, kg, seg, *, tm, rows_per_mod, feature_major=False):
    rows, d = x2.shape
    d_in = w_in_b.shape[1]
    a = qg.shape[1]
    p = d_in - 3 * a
    if rows_per_mod >= tm:
        groups = 1
        per = rows_per_mod // tm
        shift = shift.reshape(-1, 1, d)
        scale = scale.reshape(-1, 1, d)
        mod_spec = pl.BlockSpec((None, 1, d), lambda i: (i // per, 0, 0))
    else:
        assert not feature_major
        groups = tm // rows_per_mod
        mod_spec = pl.BlockSpec((groups, d), lambda i: (i, 0))
    const = lambda i: (0, 0)
    row_spec = lambda w: pl.BlockSpec((tm, w), lambda i: (i, 0))
    out_shape = [jax.ShapeDtypeStruct((rows, a), F32)] * 3 + [jax.ShapeDtypeStruct((rows, p), F32)]
    out_specs = [row_spec(a)] * 3 + [row_spec(p)]
    if feature_major:
        out_shape += [jax.ShapeDtypeStruct((rows // rows_per_mod, a, rows_per_mod), F32)] * 2
        out_specs += [pl.BlockSpec((None, a, tm), lambda i: (i // per, 0, i % per))] * 2
    return pl.pallas_call(
        functools.partial(_mixer_in_kernel, groups=groups),
        out_shape=out_shape,
        grid=(rows // tm,),
        in_specs=[row_spec(d), mod_spec, mod_spec,
                  pl.BlockSpec((1, d), const),
                  pl.BlockSpec((d, d_in), const),
                  pl.BlockSpec((1, a), const), pl.BlockSpec((1, a), const),
                  pl.BlockSpec((SEG_W, SEG_W), const)],
        out_specs=out_specs,
        compiler_params=pltpu.CompilerParams(dimension_semantics=("arbitrary",),
                                             vmem_limit_bytes=VMEM_LIMIT),
        name="mixer_in",
    )(x2, shift, scale, g1, w_in_b, qg, kg, seg)


def _prompt_attn_kernel(relb_ref, bkt_ref, q_ref, k_ref, v_ref, o_ref,
                        qd, kd, vd, bias_s, acc_s, m_s, l_s, *, blk, group):
    s_len = q_ref.shape[0]
    nbr = len(DILATED_BRANCHES)
    mid = DILATED_BRANCHES[1][1]
    msub = s_len // mid
    pair = pl.program_id(0)
    head0 = lax.broadcasted_iota(jnp.int32, (1, LANES), 1) < HEAD_DIM

    @pl.when(pl.program_id(1) == 0)
    def _():
        for bi in range(nbr):
            for hh in range(2):
                bias_s[bi, hh * blk:(hh + 1) * blk, :] = _bias_from_buckets(bkt_ref[bi], relb_ref, 2 * pair + hh)

    for bi, (_, dil) in enumerate(DILATED_BRANCHES):
        sub = s_len // dil
        for src, dst in ((q_ref, qd), (k_ref, kd), (v_ref, vd)):
            for r in range(dil):
                rows = src[...] if dil == 1 else src[pl.ds(r, sub, stride=dil), :]
                dst[bi, r * sub:(r + 1) * sub, :] = rows.astype(BF16)

    def unit(bi, base, dst, first):
        nk = blk if first else 2 * blk
        kbase = base if first else base - blk
        qb = qd[bi, pl.ds(base, blk), :]
        zero = jnp.zeros_like(qb)
        q2 = jnp.concatenate([jnp.where(head0, qb, zero), jnp.where(head0, zero, qb)], axis=0)
        s = lax.dot_general(q2, kd[bi, pl.ds(kbase, nk), :], (((1,), (1,)), ((), ())),
                            preferred_element_type=F32)
        s = s + (bias_s[bi, :, blk:] if first else bias_s[bi])
        m = jnp.max(s, axis=-1, keepdims=True)
        p = jnp.exp(s - m)
        l = jnp.sum(p, axis=-1, keepdims=True)
        acc = jnp.dot(p.astype(BF16), vd[bi, pl.ds(kbase, nk), :], preferred_element_type=F32)
        full = (blk, LANES)
        acc_s[bi, dst, :] = jnp.where(head0, acc[:blk], acc[blk:])
        m_s[bi, dst, :] = jnp.where(head0, jnp.broadcast_to(m[:blk], full), jnp.broadcast_to(m[blk:], full))
        l_s[bi, dst, :] = jnp.where(head0, jnp.broadcast_to(l[:blk], full), jnp.broadcast_to(l[blk:], full))

    for bi, (_, dil) in enumerate(DILATED_BRANCHES):
        sub = s_len // dil
        nb = sub // blk
        if dil <= mid:
            def residue(r, carry, bi=bi, sub=sub, nb=nb):
                base = pl.multiple_of(r * sub, blk)
                unit(bi, base, pl.ds(base, blk), True)

                def later(n, c):
                    b2 = pl.multiple_of(r * sub + n * blk, blk)
                    unit(bi, b2, pl.ds(b2, blk), False)
                    return c

                lax.fori_loop(1, nb, later, 0, unroll=min(group, nb - 1))
                return carry

            lax.fori_loop(0, dil, residue, 0, unroll=max(1, min(dil, group // nb)))
        else:
            step = dil // mid
            for r in range(dil):
                for n in range(nb):
                    start = (r % mid) * msub + r // mid + n * blk * step
                    unit(bi, r * sub + n * blk, pl.ds(start, blk, stride=step), n == 0)

    for r in range(mid):
        for c in range(msub // blk):
            res = slice(r * msub + c * blk, r * msub + (c + 1) * blk)
            nat = pl.ds(c * blk * mid + r, blk, stride=mid)
            rows = [nat] + [res] * (nbr - 1)
            m = [m_s[bi, rows[bi], :] for bi in range(nbr)]
            top = functools.reduce(jnp.maximum, m)
            num = jnp.zeros((blk, LANES), F32)
            den = jnp.zeros((blk, LANES), F32)
            for bi in range(nbr):
                w = jnp.exp(m[bi] - top)
                num = num + w * acc_s[bi, rows[bi], :]
                den = den + w * l_s[bi, rows[bi], :]
            o_ref[nat, :] = num / den


def _prompt_attention(q, k, v, rel_bias, blk=128, group=16):
    b, s_len, a = q.shape
    n_pairs = a // LANES
    nbr = len(DILATED_BRANCHES)
    assert all(w // d == blk and (s_len // d) % blk == 0 for w, d in DILATED_BRANCHES)
    qi = jnp.arange(blk)[:, None]
    ki = jnp.arange(2 * blk)[None, :]
    dist = qi + blk - ki
    ok = (dist >= 0) & (dist <= blk)
    bkt = jnp.stack([jnp.where(ok, _rel_bucket(jnp.clip(dist, 0, blk) * d), -1)
                     for _, d in DILATED_BRANCHES]).astype(jnp.int32)
    seq_spec = pl.BlockSpec((None, s_len, LANES), lambda p, i: (i, 0, p))
    return pl.pallas_call(
        functools.partial(_prompt_attn_kernel, blk=blk, group=group),
        out_shape=jax.ShapeDtypeStruct((b, s_len, a), F32),
        grid=(n_pairs, b),
        in_specs=[pl.BlockSpec(memory_space=pltpu.SMEM),
                  pl.BlockSpec((nbr, blk, 2 * blk), lambda p, i: (0, 0, 0)),
                  seq_spec, seq_spec, seq_spec],
        out_specs=seq_spec,
        scratch_shapes=[pltpu.VMEM((nbr, s_len, LANES), BF16)] * 3
                       + [pltpu.VMEM((nbr, 2 * blk, 2 * blk), F32)]
                       + [pltpu.VMEM((nbr, s_len, LANES), F32)] * 3,
        compiler_params=pltpu.CompilerParams(dimension_semantics=("arbitrary", "arbitrary"),
                                             vmem_limit_bytes=VMEM_LIMIT),
        name="prompt_attn",
    )(rel_bias, bkt, q, k, v)


def _sample_attn_kernel(relb_ref, bkt_ref, mult_ref, q_ref, kn_ref, vn_ref, ck_ref, cv_ref,
                        o_ref, ko_ref, vo_ref, kall, vall, bias_s, *, slab, row_chunk):
    t = q_ref.shape[0]
    a, wb = ck_ref.shape
    nk = kall.shape[1]
    pad = nk - wb
    n_slab = a // slab
    hps = slab // HEAD_DIM

    @pl.when(pl.program_id(0) == 0)
    def _():
        for h in range(a // HEAD_DIM):
            bias_s[h // hps, (h % hps) * t:(h % hps + 1) * t, :] = _bias_from_buckets(bkt_ref[...], relb_ref, h)

    def append(new_ref, cache_ref, out_ref, all_ref):
        new_t = jnp.concatenate([new_ref[...], jnp.zeros((pad - t, a), F32)], axis=0).T
        for c in range(a // row_chunk):
            rows = slice(c * row_chunk, (c + 1) * row_chunk)
            ext = jnp.concatenate([cache_ref[rows, :], new_t[rows, :]], axis=1)
            all_ref[rows, :] = ext.astype(BF16)
            out_ref[rows, :] = pltpu.roll(ext, nk - t, axis=1)[:, 0:wb]

    append(kn_ref, ck_ref, ko_ref, kall)
    append(vn_ref, cv_ref, vo_ref, vall)

    lane = lax.broadcasted_iota(jnp.int32, (hps * t, slab), 1)
    row = lax.broadcasted_iota(jnp.int32, (hps * t, slab), 0)
    own = (lane // HEAD_DIM) == (row // t)
    mult = jnp.concatenate([mult_ref[...]] * hps, axis=0)
    for sl in range(n_slab):
        cols = slice(sl * slab, (sl + 1) * slab)
        qs = jnp.concatenate([q_ref[:, cols]] * hps, axis=0)
        qs = jnp.where(own, qs, 0.0).astype(BF16)
        s = jnp.dot(qs, kall[cols, :], preferred_element_type=F32) + bias_s[sl]
        m = jnp.max(s, axis=-1, keepdims=True)
        p = jnp.exp(s - m) * mult
        l = jnp.sum(p, axis=-1, keepdims=True)
        o = lax.dot_general(p.astype(BF16), vall[cols, :], (((1,), (1,)), ((), ())),
                            preferred_element_type=F32) / l
        o = jnp.where(own, o, 0.0)
        res = o[0:t]
        for hh in range(1, hps):
            res = res + o[hh * t:(hh + 1) * t]
        o_ref[:, cols] = res


def _sample_attention(q, k_new, v_new, cache_kt, cache_vt, rel_bias, slab=256, pad=128, row_chunk=64):
    db, t, a = q.shape
    wb = cache_kt.shape[2]
    nk = wb + pad
    assert pad >= t and wb >= max(w for w, _ in DILATED_BRANCHES)
    dist = wb + jnp.arange(t)[:, None] - jnp.arange(nk)[None, :]
    mult = jnp.zeros((t, nk), jnp.int32)
    for w, d in DILATED_BRANCHES:
        mult = mult + ((dist >= 0) & (dist <= w) & (dist % d == 0)).astype(jnp.int32)
    bkt = jnp.where(mult > 0, _rel_bucket(jnp.maximum(dist, 0)), -1).astype(jnp.int32)
    hps = slab // HEAD_DIM
    new_spec = pl.BlockSpec((None, t, a), lambda i: (i, 0, 0))
    win_spec = pl.BlockSpec((None, a, wb), lambda i: (i, 0, 0))
    tab_spec = pl.BlockSpec((t, nk), lambda i: (0, 0))
    return pl.pallas_call(
        functools.partial(_sample_attn_kernel, slab=slab, row_chunk=row_chunk),
        out_shape=[jax.ShapeDtypeStruct((db, t, a), F32),
                   jax.ShapeDtypeStruct((db, a, wb), F32),
                   jax.ShapeDtypeStruct((db, a, wb), F32)],
        grid=(db,),
        in_specs=[pl.BlockSpec(memory_space=pltpu.SMEM), tab_spec, tab_spec,
                  new_spec, new_spec, new_spec, win_spec, win_spec],
        out_specs=[new_spec, win_spec, win_spec],
        scratch_shapes=[pltpu.VMEM((a, nk), BF16), pltpu.VMEM((a, nk), BF16),
                        pltpu.VMEM((a // slab, hps * t, nk), F32)],
        compiler_params=pltpu.CompilerParams(dimension_semantics=("arbitrary",),
                                             vmem_limit_bytes=VMEM_LIMIT),
        name="sample_attn",
    )(rel_bias, bkt, mult.astype(F32), q, k_new, v_new, cache_kt, cache_vt)


def _pool_kernel(u_ref, ctx_ref, wp_ref, ps_ref, o_ref, buf_a, buf_b, *, pos0):
    nb, t, p = u_ref.shape
    grp = p // len(POOL_WINDOWS)
    n = POOL_PAD + t
    ctx0 = POOL_LEAD + POOL_PAD - POOL_CTX
    buf_a[:, 0:ctx0, :] = jnp.zeros((nb, ctx0, p), F32)
    buf_b[:, 0:POOL_LEAD, :] = jnp.zeros((nb, POOL_LEAD, p), F32)
    buf_a[:, ctx0:ctx0 + POOL_CTX, :] = jnp.broadcast_to(ctx_ref[...], (nb, POOL_CTX, p))
    buf_a[:, ctx0 + POOL_CTX:, :] = u_ref[...]

    src, dst = buf_a, buf_b
    where = []
    for g, w in enumerate(POOL_WINDOWS):
        assert w == 2 ** (g + 1)
        cols = slice(g * grp, p)
        dst[:, POOL_LEAD:POOL_LEAD + n, cols] = (src[:, POOL_LEAD:POOL_LEAD + n, cols]
                                                  + src[:, POOL_LEAD - w // 2:POOL_LEAD - w // 2 + n, cols])
        where.append(dst)
        src, dst = dst, src

    pos = (pos0 + lax.broadcasted_iota(jnp.int32, (1, t, 1), 1)).astype(F32)
    new0 = POOL_LEAD + POOL_PAD
    for g, w in enumerate(POOL_WINDOWS):
        cols = slice(g * grp, (g + 1) * grp)
        win = where[g][:, new0:new0 + t, cols]
        cnt = jnp.minimum(float(w), pos + 1.0)
        pooled = (win / cnt - u_ref[:, :, cols]).reshape(nb * t, grp)
        y = jnp.dot(pooled.astype(BF16), wp_ref[g], preferred_element_type=F32)
        o_ref[:, :, cols] = (y * ps_ref[:, cols]).reshape(nb, t, grp)


def _pool_mix(u, ctx, w_pool_b, pool_scale, *, seqs_per_step, pos0):
    nb, t, p = u.shape
    g = len(POOL_WINDOWS)
    bb = seqs_per_step
    u_spec = pl.BlockSpec((bb, t, p), lambda i: (i, 0, 0))
    if ctx.shape[0] == 1:
        ctx_spec = pl.BlockSpec((1, POOL_CTX, p), lambda i: (0, 0, 0))
    else:
        ctx_spec = pl.BlockSpec((bb, POOL_CTX, p), lambda i: (i, 0, 0))
    return pl.pallas_call(
        functools.partial(_pool_kernel, pos0=pos0),
        out_shape=jax.ShapeDtypeStruct((nb, t, p), F32),
        grid=(nb // bb,),
        in_specs=[u_spec, ctx_spec,
                  pl.BlockSpec((g, p // g, p // g), lambda i: (0, 0, 0)),
                  pl.BlockSpec((1, p), lambda i: (0, 0))],
        out_specs=u_spec,
        scratch_shapes=[pltpu.VMEM((bb, POOL_LEAD + POOL_PAD + t, p), F32)] * 2,
        compiler_params=pltpu.CompilerParams(dimension_semantics=("arbitrary",),
                                             vmem_limit_bytes=VMEM_LIMIT),
        name="pool_mix",
    )(u, ctx, w_pool_b, pool_scale)


def _tail_kernel(x_ref, attn_ref, pool_ref, gate1_ref, shift2_ref, scale2_ref, gate2_ref, g2_ref,
                 wout_ref, wup_ref, wdown_ref, y_ref, *, groups, ff_chunk):
    a = attn_ref.shape[1]
    d_ff = wup_ref.shape[1]
    mix = jnp.dot(attn_ref[...].astype(BF16), wout_ref[0:a, :], preferred_element_type=F32)
    mix = mix + jnp.dot(pool_ref[...].astype(BF16), wout_ref[a:, :], preferred_element_type=F32)
    x1 = x_ref[...] + _per_group(gate1_ref[...], mix, groups)
    hb = _modulated_norm(x1, g2_ref[...], shift2_ref[...], scale2_ref[...], groups).astype(BF16)
    f = jnp.zeros(x1.shape, F32)
    for c in range(d_ff // ff_chunk):
        cols = slice(c * ff_chunk, (c + 1) * ff_chunk)
        act = jnp.maximum(jnp.dot(hb, wup_ref[:, cols], preferred_element_type=F32), 0.0)
        f = f + jnp.dot((act * act).astype(BF16), wdown_ref[cols, :], preferred_element_type=F32)
    y_ref[...] = x1 + _per_group(gate2_ref[...], f, groups)


def _layer_tail(x2, attn2, pool2, gate1, shift2, scale2, gate2, g2, w_out_b, w_up_b, w_down_b,
                *, tm, rows_per_mod, ff_chunk=1024):
    rows, d = x2.shape
    a = attn2.shape[1]
    p = pool2.shape[1]
    d_ff = w_up_b.shape[1]
    mods = [gate1, shift2, scale2, gate2]
    if rows_per_mod >= tm:
        groups = 1
        per = rows_per_mod // tm
        mods = [m.reshape(-1, 1, d) for m in mods]
        mod_spec = pl.BlockSpec((None, 1, d), lambda i: (i // per, 0, 0))
    else:
        groups = tm // rows_per_mod
        mod_spec = pl.BlockSpec((groups, d), lambda i: (i, 0))
    const = lambda i: (0, 0)
    row_spec = lambda w: pl.BlockSpec((tm, w), lambda i: (i, 0))
    resident = lambda shape: pl.BlockSpec(shape, const, pipeline_mode=pl.Buffered(1))
    return pl.pallas_call(
        functools.partial(_tail_kernel, groups=groups, ff_chunk=ff_chunk),
        out_shape=jax.ShapeDtypeStruct((rows, d), F32),
        grid=(rows // tm,),
        in_specs=[row_spec(d), row_spec(a), row_spec(p), mod_spec, mod_spec, mod_spec, mod_spec,
                  pl.BlockSpec((1, d), const),
                  resident((a + p, d)), resident((d, d_ff)), resident((d_ff, d))],
        out_specs=row_spec(d),
        compiler_params=pltpu.CompilerParams(dimension_semantics=("arbitrary",),
                                             vmem_limit_bytes=VMEM_LIMIT),
        name="layer_tail",
    )(x2, attn2, pool2, *mods, g2, w_out_b, w_up_b, w_down_b)


def kernel(x_prompt, x_sample, c_prompt, c_sample, cache_k, cache_v, state_pool, w_ada, b_ada, norm1_g, norm2_g, w_in, q_norm_g, k_norm_g, rel_bias, w_pool, pool_scale, w_out, w_up, w_down):
    b, s_len, d = x_prompt.shape
    db, t, _ = x_sample.shape
    depth = w_ada.shape[0]
    assert depth == 1, "single-layer step"
    n_heads = rel_bias.shape[1]
    a = n_heads * HEAD_DIM
    wb = cache_k.shape[2]
    l = 0

    ada = _adaln(jnp.concatenate([c_prompt, c_sample], axis=0), w_ada[l], b_ada[l])
    mods = [ada[:, i * d:(i + 1) * d] for i in range(N_ADA)]
    mp = [m[:b] for m in mods]
    msm = [m[b:] for m in mods]

    w_in_b = w_in[l].astype(BF16)
    w_out_b = w_out[l].astype(BF16)
    w_up_b = w_up[l].astype(BF16)
    w_down_b = w_down[l].astype(BF16)
    w_pool_b = w_pool[l].astype(BF16)
    g1 = norm1_g[l].reshape(1, d)
    g2 = norm2_g[l].reshape(1, d)
    qg = jnp.tile(q_norm_g[l], n_heads).reshape(1, a)
    kg = jnp.tile(k_norm_g[l], n_heads).reshape(1, a)
    ps = pool_scale[l].reshape(1, -1)
    seg_i = jnp.arange(SEG_W) // HEAD_DIM
    seg = jnp.where(seg_i[:, None] == seg_i[None, :], 1.0 / HEAD_DIM, 0.0).astype(BF16)

    xp2 = x_prompt.reshape(b * s_len, d)
    q, k, v, u, kt, vt = _mixer_in(xp2, mp[0], mp[1], g1, w_in_b, qg, kg, seg, tm=512, rows_per_mod=s_len,
                                   feature_major=True)
    attn = _prompt_attention(q.reshape(b, s_len, a), k.reshape(b, s_len, a), v.reshape(b, s_len, a), rel_bias)
    u3 = u.reshape(b, s_len, -1)
    pool = _pool_mix(u3, jnp.zeros((1, POOL_CTX, u3.shape[2]), F32), w_pool_b, ps, seqs_per_step=1, pos0=0)
    y_prompt = _layer_tail(xp2, attn.reshape(b * s_len, a), pool.reshape(b * s_len, -1),
                           mp[2], mp[3], mp[4], mp[5], g2, w_out_b, w_up_b, w_down_b,
                           tm=512, rows_per_mod=s_len).reshape(b, s_len, d)
    keep = min(max(w for w, _ in DILATED_BRANCHES), s_len)
    win = lambda c: jnp.transpose(c.reshape(b, n_heads, HEAD_DIM, s_len), (0, 3, 1, 2))[None, :, s_len - keep:]
    k_win_prompt = win(kt)
    v_win_prompt = win(vt)
    pool_prompt = u3[None, :, s_len - POOL_CTX:]

    xs2 = x_sample.reshape(db * t, d)
    qs, ks, vs, us = _mixer_in(xs2, msm[0], msm[1], g1, w_in_b, qg, kg, seg, tm=256, rows_per_mod=t)
    to_fm = lambda c: jnp.transpose(c, (0, 2, 3, 1)).reshape(db, a, wb)
    from_fm = lambda c: jnp.transpose(c.reshape(db, n_heads, HEAD_DIM, wb), (0, 3, 1, 2))[None]
    attn_s, k_win, v_win = _sample_attention(
        qs.reshape(db, t, a), ks.reshape(db, t, a), vs.reshape(db, t, a),
        to_fm(cache_k[l]), to_fm(cache_v[l]), rel_bias)
    us3 = us.reshape(db, t, -1)
    pool_s = _pool_mix(us3, state_pool[l], w_pool_b, ps, seqs_per_step=db, pos0=PAST_LEN)
    y_sample = _layer_tail(xs2, attn_s.reshape(db * t, a), pool_s.reshape(db * t, -1),
                           msm[2], msm[3], msm[4], msm[5], g2, w_out_b, w_up_b, w_down_b,
                           tm=256, rows_per_mod=t).reshape(db, t, d)
    k_win_sample = from_fm(k_win)
    v_win_sample = from_fm(v_win)
    pool_sample = jnp.concatenate([state_pool[l], us3], axis=1)[None, :, -POOL_CTX:]

    return (y_prompt, y_sample, k_win_prompt, v_win_prompt, pool_prompt,
            k_win_sample, v_win_sample, pool_sample)
```

```python
import functools
import math

import jax
import jax.numpy as jnp
from jax import lax
from jax.experimental import pallas as pl
from jax.experimental.pallas import tpu as pltpu

F32 = jnp.float32
BF16 = jnp.bfloat16

HEAD_DIM = 64
DILATED_BRANCHES = ((128, 1), (512, 4), (2048, 16))
NUM_BUCKETS = 32
MAX_DISTANCE = 2048
POOL_WINDOWS = (2, 4, 8, 16)
PAST_LEN = 8192
POOL_CTX = max(POOL_WINDOWS) - 1
POOL_PAD = 16
POOL_LEAD = max(POOL_WINDOWS) // 2
N_ADA = 6
EPS = 1e-6
NEG_INF = -1e30

LANES = 128
SEG_W = 256
VMEM_LIMIT = 56 * 1024 * 1024
VMEM_LIMIT_BIG = 62 * 1024 * 1024


def _rel_bucket(dist):
    exact = NUM_BUCKETS // 2
    d = jnp.maximum(dist.astype(F32), 1.0)
    large = exact + (jnp.log(d / exact) / math.log(MAX_DISTANCE / exact)
                     * (NUM_BUCKETS - exact)).astype(jnp.int32)
    large = jnp.minimum(large, NUM_BUCKETS - 1)
    return jnp.where(dist < exact, dist, large)


def _bias_from_buckets(bkt, relb_ref, head):
    out = jnp.full(bkt.shape, NEG_INF, F32)
    for b in range(NUM_BUCKETS):
        out = jnp.where(bkt == b, relb_ref[b, head], out)
    return out


def _adaln_kernel(c_ref, w_ref, b_ref, o_ref):
    c = c_ref[...]
    s = c / (1.0 + jnp.exp(-c))
    o_ref[...] = jnp.dot(s, w_ref[...], preferred_element_type=F32,
                         precision=lax.Precision.HIGHEST) + b_ref[...]


def _adaln(c, w, b, tn=1024):
    m, d = c.shape
    n = w.shape[1]
    return pl.pallas_call(
        _adaln_kernel,
        out_shape=jax.ShapeDtypeStruct((m, n), F32),
        grid=(n // tn,),
        in_specs=[pl.BlockSpec((m, d), lambda j: (0, 0)),
                  pl.BlockSpec((d, tn), lambda j: (0, j)),
                  pl.BlockSpec((1, tn), lambda j: (0, j))],
        out_specs=pl.BlockSpec((m, tn), lambda j: (0, j)),
        compiler_params=pltpu.CompilerParams(dimension_semantics=("arbitrary",),
                                             vmem_limit_bytes=VMEM_LIMIT),
        name="adaln",
    )(c, w, b.reshape(1, n))


def _modulated_norm(x, g, shift, scale, groups):
    tm, d = x.shape
    ms = jnp.mean(x * x, axis=-1, keepdims=True)
    y = x * lax.rsqrt(ms + EPS) * g
    if groups == 1:
        return y * (1.0 + scale) + shift
    y3 = y.reshape(groups, tm // groups, d)
    return (y3 * (1.0 + scale[:, None, :]) + shift[:, None, :]).reshape(tm, d)


def _per_group(v, x, groups):
    tm, d = x.shape
    if groups == 1:
        return v * x
    return (v[:, None, :] * x.reshape(groups, tm // groups, d)).reshape(tm, d)


def _mixer_in_kernel(x_ref, shift_ref, scale_ref, g1_ref, w_ref, qg_ref, kg_ref, seg_ref,
                     q_ref, k_ref, v_ref, u_ref, *fm_refs, groups):
    a = q_ref.shape[1]
    h = _modulated_norm(x_ref[...], g1_ref[...], shift_ref[...], scale_ref[...], groups)
    hb = h.astype(BF16)

    def head_norm(z, g):
        parts = []
        for c in range(a // SEG_W):
            zc = z[:, c * SEG_W:(c + 1) * SEG_W]
            ms = jnp.dot((zc * zc).astype(BF16), seg_ref[...], preferred_element_type=F32)
            parts.append(zc * lax.rsqrt(ms + EPS))
        return jnp.concatenate(parts, axis=-1) * g

    zq = jnp.dot(hb, w_ref[:, 0:a], preferred_element_type=F32)
    q_ref[...] = head_norm(zq, qg_ref[...]) * (HEAD_DIM ** -0.5)
    zk = jnp.dot(hb, w_ref[:, a:2 * a], preferred_element_type=F32)
    k = head_norm(zk, kg_ref[...])
    k_ref[...] = k
    v = jnp.dot(hb, w_ref[:, 2 * a:3 * a], preferred_element_type=F32)
    v_ref[...] = v
    u_ref[...] = jnp.dot(hb, w_ref[:, 3 * a:], preferred_element_type=F32)
    if fm_refs:
        kt_ref, vt_ref = fm_refs
        kt_ref[...] = k.T
        vt_ref[...] = v.T


def _mixer_in(x2, shift, scale, g1, w_in_b, qg, kg, seg, *, tm, rows_per_mod, feature_major=False):
    rows, d = x2.shape
    d_in = w_in_b.shape[1]
    a = qg.shape[1]
    p = d_in - 3 * a
    if rows_per_mod >= tm:
        groups = 1
        per = rows_per_mod // tm
        shift = shift.reshape(-1, 1, d)
        scale = scale.reshape(-1, 1, d)
        mod_spec = pl.BlockSpec((None, 1, d), lambda i: (i // per, 0, 0))
    else:
        assert not feature_major
        groups = tm // rows_per_mod
        mod_spec = pl.BlockSpec((groups, d), lambda i: (i, 0))
    const = lambda i: (0, 0)
    row_spec = lambda w: pl.BlockSpec((tm, w), lambda i: (i, 0))
    out_shape = [jax.ShapeDtypeStruct((rows, a), F32)] * 3 + [jax.ShapeDtypeStruct((rows, p), F32)]
    out_specs = [row_spec(a)] * 3 + [row_spec(p)]
    if feature_major:
        out_shape += [jax.ShapeDtypeStruct((rows // rows_per_mod, a, rows_per_mod), F32)] * 2
        out_specs += [pl.BlockSpec((None, a, tm), lambda i: (i // per, 0, i % per))] * 2
    return pl.pallas_call(
        functools.partial(_mixer_in_kernel, groups=groups),
        out_shape=out_shape,
        grid=(rows // tm,),
        in_specs=[row_spec(d), mod_spec, mod_spec,
                  pl.BlockSpec((1, d), const),
                  pl.BlockSpec((d, d_in), const),
                  pl.BlockSpec((1, a), const), pl.BlockSpec((1, a), const),
                  pl.BlockSpec((SEG_W, SEG_W), const)],
        out_specs=out_specs,
        compiler_params=pltpu.CompilerParams(dimension_semantics=("arbitrary",),
                                             vmem_limit_bytes=VMEM_LIMIT),
        name="mixer_in",
    )(x2, shift, scale, g1, w_in_b, qg, kg, seg)


def _prompt_attn_kernel(relb_ref, bkt_ref, q_ref, k_ref, v_ref, o_ref,
                        qd, kd, vd, bias_s, acc_s, m_s, l_s, *, blk, group):
    s_len = q_ref.shape[0]
    nbr = len(DILATED_BRANCHES)
    mid = DILATED_BRANCHES[1][1]
    msub = s_len // mid
    pair = pl.program_id(0)
    head0 = lax.broadcasted_iota(jnp.int32, (1, LANES), 1) < HEAD_DIM

    @pl.when(pl.program_id(1) == 0)
    def _():
        for bi in range(nbr):
            for hh in range(2):
                bias_s[bi, hh * blk:(hh + 1) * blk, :] = _bias_from_buckets(bkt_ref[bi], relb_ref, 2 * pair + hh)

    for bi, (_, dil) in enumerate(DILATED_BRANCHES):
        sub = s_len // dil
        for src, dst in ((q_ref, qd), (k_ref, kd), (v_ref, vd)):
            for r in range(dil):
                rows = src[...] if dil == 1 else src[pl.ds(r, sub, stride=dil), :]
                dst[bi, r * sub:(r + 1) * sub, :] = rows.astype(BF16)

    def unit(bi, base, dst, first):
        nk = blk if first else 2 * blk
        kbase = base if first else base - blk
        qb = qd[bi, pl.ds(base, blk), :]
        zero = jnp.zeros_like(qb)
        q2 = jnp.concatenate([jnp.where(head0, qb, zero), jnp.where(head0, zero, qb)], axis=0)
        s = lax.dot_general(q2, kd[bi, pl.ds(kbase, nk), :], (((1,), (1,)), ((), ())),
                            preferred_element_type=F32)
        s = s + (bias_s[bi, :, blk:] if first else bias_s[bi])
        m = jnp.max(s, axis=-1, keepdims=True)
        p = jnp.exp(s - m)
        l = jnp.sum(p, axis=-1, keepdims=True)
        acc = jnp.dot(p.astype(BF16), vd[bi, pl.ds(kbase, nk), :], preferred_element_type=F32)
        full = (blk, LANES)
        acc_s[bi, dst, :] = jnp.where(head0, acc[:blk], acc[blk:])
        m_s[bi, dst, :] = jnp.where(head0, jnp.broadcast_to(m[:blk], full), jnp.broadcast_to(m[blk:], full))
        l_s[bi, dst, :] = jnp.where(head0, jnp.broadcast_to(l[:blk], full), jnp.broadcast_to(l[blk:], full))

    for bi, (_, dil) in enumerate(DILATED_BRANCHES):
        sub = s_len // dil
        nb = sub // blk
        if dil <= mid:
            def residue(r, carry, bi=bi, sub=sub, nb=nb):
                base = pl.multiple_of(r * sub, blk)
                unit(bi, base, pl.ds(base, blk), True)

                def later(n, c):
                    b2 = pl.multiple_of(r * sub + n * blk, blk)
                    unit(bi, b2, pl.ds(b2, blk), False)
                    return c

                lax.fori_loop(1, nb, later, 0, unroll=min(group, nb - 1))
                return carry

            lax.fori_loop(0, dil, residue, 0, unroll=max(1, min(dil, group // nb)))
        else:
            step = dil // mid
            for r in range(dil):
                for n in range(nb):
                    start = (r % mid) * msub + r // mid + n * blk * step
                    unit(bi, r * sub + n * blk, pl.ds(start, blk, stride=step), n == 0)

    for r in range(mid):
        for c in range(msub // blk):
            res = slice(r * msub + c * blk, r * msub + (c + 1) * blk)
            nat = pl.ds(c * blk * mid + r, blk, stride=mid)
            rows = [nat] + [res] * (nbr - 1)
            m = [m_s[bi, rows[bi], :] for bi in range(nbr)]
            top = functools.reduce(jnp.maximum, m)
            num = jnp.zeros((blk, LANES), F32)
            den = jnp.zeros((blk, LANES), F32)
            for bi in range(nbr):
                w = jnp.exp(m[bi] - top)
                num = num + w * acc_s[bi, rows[bi], :]
                den = den + w * l_s[bi, rows[bi], :]
            o_ref[nat, :] = num / den


def _prompt_attention(q, k, v, rel_bias, blk=128, group=16):
    b, s_len, a = q.shape
    n_pairs = a // LANES
    nbr = len(DILATED_BRANCHES)
    assert all(w // d == blk and (s_len // d) % blk == 0 for w, d in DILATED_BRANCHES)
    qi = jnp.arange(blk)[:, None]
    ki = jnp.arange(2 * blk)[None, :]
    dist = qi + blk - ki
    ok = (dist >= 0) & (dist <= blk)
    bkt = jnp.stack([jnp.where(ok, _rel_bucket(jnp.clip(dist, 0, blk) * d), -1)
                     for _, d in DILATED_BRANCHES]).astype(jnp.int32)
    seq_spec = pl.BlockSpec((None, s_len, LANES), lambda p, i: (i, 0, p))
    return pl.pallas_call(
        functools.partial(_prompt_attn_kernel, blk=blk, group=group),
        out_shape=jax.ShapeDtypeStruct((b, s_len, a), F32),
        grid=(n_pairs, b),
        in_specs=[pl.BlockSpec(memory_space=pltpu.SMEM),
                  pl.BlockSpec((nbr, blk, 2 * blk), lambda p, i: (0, 0, 0)),
                  seq_spec, seq_spec, seq_spec],
        out_specs=seq_spec,
        scratch_shapes=[pltpu.VMEM((nbr, s_len, LANES), BF16)] * 3
                       + [pltpu.VMEM((nbr, 2 * blk, 2 * blk), F32)]
                       + [pltpu.VMEM((nbr, s_len, LANES), F32)] * 3,
        compiler_params=pltpu.CompilerParams(dimension_semantics=("arbitrary", "arbitrary"),
                                             vmem_limit_bytes=VMEM_LIMIT),
        name="prompt_attn",
    )(rel_bias, bkt, q, k, v)


def _sample_tables(t, wb, nk):
    dist = wb + jnp.arange(t)[:, None] - jnp.arange(nk)[None, :]
    mult = jnp.zeros((t, nk), jnp.int32)
    for w, d in DILATED_BRANCHES:
        mult = mult + ((dist >= 0) & (dist <= w) & (dist % d == 0)).astype(jnp.int32)
    bkt = jnp.where(mult > 0, _rel_bucket(jnp.maximum(dist, 0)), -1).astype(jnp.int32)
    return bkt, mult.astype(F32)


def _sample_slab(bias, mult, q_ref, kn_ref, vn_ref, ck_ref, cv_ref, o_ref, ko_ref, vo_ref, kall, vall,
                 row_chunk):
    t, slab = q_ref.shape
    wb = ck_ref.shape[1]
    nk = kall.shape[1]
    pad = nk - wb
    hps = slab // HEAD_DIM

    def append(new_ref, cache_ref, out_ref, all_ref):
        new_t = jnp.concatenate([new_ref[...], jnp.zeros((pad - t, slab), F32)], axis=0).T
        for c in range(slab // row_chunk):
            rows = slice(c * row_chunk, (c + 1) * row_chunk)
            ext = jnp.concatenate([cache_ref[rows, :], new_t[rows, :]], axis=1)
            all_ref[rows, :] = ext.astype(BF16)
            out_ref[rows, :] = pltpu.roll(ext, nk - t, axis=1)[:, 0:wb]

    append(kn_ref, ck_ref, ko_ref, kall)
    append(vn_ref, cv_ref, vo_ref, vall)

    lane = lax.broadcasted_iota(jnp.int32, (hps * t, slab), 1)
    row = lax.broadcasted_iota(jnp.int32, (hps * t, slab), 0)
    own = (lane // HEAD_DIM) == (row // t)
    qs = jnp.where(own, jnp.concatenate([q_ref[...]] * hps, axis=0), 0.0).astype(BF16)
    s = jnp.dot(qs, kall[...], preferred_element_type=F32) + bias
    m = jnp.max(s, axis=-1, keepdims=True)
    p = jnp.exp(s - m) * jnp.concatenate([mult] * hps, axis=0)
    l = jnp.sum(p, axis=-1, keepdims=True)
    o = lax.dot_general(p.astype(BF16), vall[...], (((1,), (1,)), ((), ())),
                        preferred_element_type=F32) / l
    o = jnp.where(own, o, 0.0)
    res = o[0:t]
    for hh in range(1, hps):
        res = res + o[hh * t:(hh + 1) * t]
    o_ref[...] = res


def _tail_sample_kernel(x_ref, attn_ref, pool_ref, gate1_ref, shift2_ref, scale2_ref, gate2_ref, g2_ref,
                        wout_ref, wup_ref, wdown_ref,
                        relb_ref, bkt_ref, mult_ref, q_ref, kn_ref, vn_ref, ck_ref, cv_ref,
                        y_ref, o_ref, ko_ref, vo_ref,
                        hb_s, acc_s, kall, vall, bias_s, *, n_slab, row_chunk):
    i = pl.program_id(0)
    j = pl.program_id(1)
    n_phase = pl.num_programs(1)
    a = attn_ref.shape[1]
    t, slab = q_ref.shape
    hps = slab // HEAD_DIM

    @pl.when((i == 0) & (j == 0))
    def _():
        for h in range(n_slab * hps):
            bias_s[h // hps, (h % hps) * t:(h % hps + 1) * t, :] = _bias_from_buckets(bkt_ref[...], relb_ref, h)

    @pl.when(j == 0)
    def _():
        mix = jnp.dot(attn_ref[...].astype(BF16), wout_ref[0:a, :], preferred_element_type=F32)
        mix = mix + jnp.dot(pool_ref[...].astype(BF16), wout_ref[a:, :], preferred_element_type=F32)
        x1 = x_ref[...] + gate1_ref[...] * mix
        y_ref[...] = x1
        hb_s[...] = _modulated_norm(x1, g2_ref[...], shift2_ref[...], scale2_ref[...], 1).astype(BF16)
        acc_s[...] = jnp.zeros(acc_s.shape, F32)

    sl = (i * n_phase + j) % n_slab
    _sample_slab(bias_s[sl], mult_ref[...], q_ref, kn_ref, vn_ref, ck_ref, cv_ref,
                 o_ref, ko_ref, vo_ref, kall, vall, row_chunk)
    act =jnp.maximum(jnp.dot(hb_s[...], wup_ref[j], preferred_element_type=F32), 0.0)
    acc_s[...] += jnp.dot((act * act).astype(BF16), wdown_ref[j], preferred_element_type=F32)

    @pl.when(j == n_phase - 1)
    def _():
        y_ref[...] = y_ref[...] + gate2_ref[...] * acc_s[...]


def _tail_and_sample_attention(x2, attn2, pool2, gate1, shift2, scale2, gate2, g2, w_out_b, w_up_b, w_down_b,
                               q, k_new, v_new, cache_kt, cache_vt, rel_bias,
                               *, tm, rows_per_mod, ff_chunk=1024, slab=256, pad=128, row_chunk=64):
    rows, d = x2.shape
    a = attn2.shape[1]
    p = pool2.shape[1]
    d_ff = w_up_b.shape[1]
    db, t, _ = q.shape
    wb = cache_kt.shape[2]
    nk = wb + pad
    n_slab = a // slab
    n_phase = d_ff // ff_chunk
    n_tiles = rows // tm
    hps = slab // HEAD_DIM
    assert pad >= t and wb >= max(w for w, _ in DILATED_BRANCHES)
    assert n_tiles * n_phase == db * n_slab and rows_per_mod % tm == 0
    per = rows_per_mod // tm
    bkt, mult = _sample_tables(t, wb, nk)
    wup3 = jnp.transpose(w_up_b.reshape(d, n_phase, ff_chunk), (1, 0, 2))
    wdown3 = w_down_b.reshape(n_phase, ff_chunk, d)
    mods = [m.reshape(-1, 1, d) for m in (gate1, shift2, scale2, gate2)]

    row_spec = lambda w: pl.BlockSpec((tm, w), lambda i, j: (i, 0))
    mod_spec = pl.BlockSpec((None, 1, d), lambda i, j: (i // per, 0, 0))
    resident = lambda shape: pl.BlockSpec(shape, lambda i, j: (0,) * len(shape), pipeline_mode=pl.Buffered(1))
    seq = lambda i, j: (i * n_phase + j) // n_slab
    sl = lambda i, j: (i * n_phase + j) % n_slab
    new_spec = pl.BlockSpec((None, t, slab), lambda i, j: (seq(i, j), 0, sl(i, j)))
    win_spec = pl.BlockSpec((None, slab, wb), lambda i, j: (seq(i, j), sl(i, j), 0))
    tab_spec = pl.BlockSpec((t, nk), lambda i, j: (0, 0))
    return pl.pallas_call(
        functools.partial(_tail_sample_kernel, n_slab=n_slab, row_chunk=row_chunk),
        out_shape=[jax.ShapeDtypeStruct((rows, d), F32),
                   jax.ShapeDtypeStruct((db, t, a), F32),
                   jax.ShapeDtypeStruct((db, a, wb), F32),
                   jax.ShapeDtypeStruct((db, a, wb), F32)],
        grid=(n_tiles, n_phase),
        in_specs=[row_spec(d), row_spec(a), row_spec(p), mod_spec, mod_spec, mod_spec, mod_spec,
                  pl.BlockSpec((1, d), lambda i, j: (0, 0)),
                  resident((a + p, d)), resident((n_phase, d, ff_chunk)), resident((n_phase, ff_chunk, d)),
                  pl.BlockSpec(memory_space=pltpu.SMEM), tab_spec, tab_spec,
                  new_spec, new_spec, new_spec, win_spec, win_spec],
        out_specs=[row_spec(d), new_spec, win_spec, win_spec],
        scratch_shapes=[pltpu.VMEM((tm, d), BF16), pltpu.VMEM((tm, d), F32),
                        pltpu.VMEM((slab, nk), BF16), pltpu.VMEM((slab, nk), BF16),
                        pltpu.VMEM((n_slab, hps * t, nk), F32)],
        compiler_params=pltpu.CompilerParams(dimension_semantics=("arbitrary", "arbitrary"),
                                             vmem_limit_bytes=VMEM_LIMIT_BIG),
        name="tail_sample",
    )(x2, attn2, pool2, *mods, g2, w_out_b, wup3, wdown3,
      rel_bias, bkt, mult, q, k_new, v_new, cache_kt, cache_vt)


def _pool_kernel(u_ref, ctx_ref, wp_ref, ps_ref, o_ref, buf_a, buf_b, *, pos0):
    nb, t, p = u_ref.shape
    grp = p // len(POOL_WINDOWS)
    n = POOL_PAD + t
    ctx0 = POOL_LEAD + POOL_PAD - POOL_CTX
    buf_a[:, 0:ctx0, :] = jnp.zeros((nb, ctx0, p), F32)
    buf_b[:, 0:POOL_LEAD, :] = jnp.zeros((nb, POOL_LEAD, p), F32)
    buf_a[:, ctx0:ctx0 + POOL_CTX, :] = jnp.broadcast_to(ctx_ref[...], (nb, POOL_CTX, p))
    buf_a[:, ctx0 + POOL_CTX:, :] = u_ref[...]

    src, dst = buf_a, buf_b
    where = []
    for g, w in enumerate(POOL_WINDOWS):
        assert w == 2 ** (g + 1)
        cols = slice(g * grp, p)
        dst[:, POOL_LEAD:POOL_LEAD + n, cols] = (src[:, POOL_LEAD:POOL_LEAD + n, cols]
                                                  + src[:, POOL_LEAD - w // 2:POOL_LEAD - w // 2 + n, cols])
        where.append(dst)
        src, dst = dst, src

    pos = (pos0 + lax.broadcasted_iota(jnp.int32, (1, t, 1), 1)).astype(F32)
    new0 = POOL_LEAD + POOL_PAD
    for g, w in enumerate(POOL_WINDOWS):
        cols = slice(g * grp, (g + 1) * grp)
        win = where[g][:, new0:new0 + t, cols]
        cnt = jnp.minimum(float(w), pos + 1.0)
        pooled = (win / cnt - u_ref[:, :, cols]).reshape(nb * t, grp)
        y = jnp.dot(pooled.astype(BF16), wp_ref[g], preferred_element_type=F32)
        o_ref[:, :, cols] = (y * ps_ref[:, cols]).reshape(nb, t, grp).astype(o_ref.dtype)


def _pool_mix(u, ctx, w_pool_b, pool_scale, *, seqs_per_step, pos0):
    nb, t, p = u.shape
    g = len(POOL_WINDOWS)
    bb = seqs_per_step
    u_spec = pl.BlockSpec((bb, t, p), lambda i: (i, 0, 0))
    if ctx.shape[0] == 1:
        ctx_spec = pl.BlockSpec((1, POOL_CTX, p), lambda i: (0, 0, 0))
    else:
        ctx_spec = pl.BlockSpec((bb, POOL_CTX, p), lambda i: (i, 0, 0))
    return pl.pallas_call(
        functools.partial(_pool_kernel, pos0=pos0),
        out_shape=jax.ShapeDtypeStruct((nb, t, p), BF16),
        grid=(nb // bb,),
        in_specs=[u_spec, ctx_spec,
                  pl.BlockSpec((g, p // g, p // g), lambda i: (0, 0, 0)),
                  pl.BlockSpec((1, p), lambda i: (0, 0))],
        out_specs=u_spec,
        scratch_shapes=[pltpu.VMEM((bb, POOL_LEAD + POOL_PAD + t, p), F32)] * 2,
        compiler_params=pltpu.CompilerParams(dimension_semantics=("arbitrary",),
                                             vmem_limit_bytes=VMEM_LIMIT),
        name="pool_mix",
    )(u, ctx, w_pool_b, pool_scale)


def _tail_kernel(x_ref, attn_ref, pool_ref, gate1_ref, shift2_ref, scale2_ref, gate2_ref, g2_ref,
                 wout_ref, wup_ref, wdown_ref, y_ref, *, groups, ff_chunk):
    a = attn_ref.shape[1]
    d_ff = wup_ref.shape[1]
    mix = jnp.dot(attn_ref[...].astype(BF16), wout_ref[0:a, :], preferred_element_type=F32)
    mix = mix + jnp.dot(pool_ref[...].astype(BF16), wout_ref[a:, :], preferred_element_type=F32)
    x1 = x_ref[...] + _per_group(gate1_ref[...], mix, groups)
    hb = _modulated_norm(x1, g2_ref[...], shift2_ref[...], scale2_ref[...], groups).astype(BF16)
    f = jnp.zeros(x1.shape, F32)
    for c in range(d_ff // ff_chunk):
        cols = slice(c * ff_chunk, (c + 1) * ff_chunk)
        act = jnp.maximum(jnp.dot(hb, wup_ref[:, cols], preferred_element_type=F32), 0.0)
        f = f + jnp.dot((act * act).astype(BF16), wdown_ref[cols, :], preferred_element_type=F32)
    y_ref[...] = x1 + _per_group(gate2_ref[...], f, groups)


def _layer_tail(x2, attn2, pool2, gate1, shift2, scale2, gate2, g2, w_out_b, w_up_b, w_down_b,
                *, tm, rows_per_mod, ff_chunk=1024):
    rows, d = x2.shape
    a = attn2.shape[1]
    p = pool2.shape[1]
    d_ff = w_up_b.shape[1]
    mods = [gate1, shift2, scale2, gate2]
    if rows_per_mod >= tm:
        groups = 1
        per = rows_per_mod // tm
        mods = [m.reshape(-1, 1, d) for m in mods]
        mod_spec = pl.BlockSpec((None, 1, d), lambda i: (i // per, 0, 0))
    else:
        groups = tm // rows_per_mod
        mod_spec = pl.BlockSpec((groups, d), lambda i: (i, 0))
    const = lambda i: (0, 0)
    row_spec = lambda w: pl.BlockSpec((tm, w), lambda i: (i, 0))
    resident = lambda shape: pl.BlockSpec(shape, const, pipeline_mode=pl.Buffered(1))
    return pl.pallas_call(
        functools.partial(_tail_kernel, groups=groups, ff_chunk=ff_chunk),
        out_shape=jax.ShapeDtypeStruct((rows, d), F32),
        grid=(rows // tm,),
        in_specs=[row_spec(d), row_spec(a), row_spec(p), mod_spec, mod_spec, mod_spec, mod_spec,
                  pl.BlockSpec((1, d), const),
                  resident((a + p, d)), resident((d, d_ff)), resident((d_ff, d))],
        out_specs=row_spec(d),
        compiler_params=pltpu.CompilerParams(dimension_semantics=("arbitrary",),
                                             vmem_limit_bytes=VMEM_LIMIT),
        name="layer_tail",
    )(x2, attn2, pool2, *mods, g2, w_out_b, w_up_b, w_down_b)


def kernel(x_prompt, x_sample, c_prompt, c_sample, cache_k, cache_v, state_pool, w_ada, b_ada, norm1_g, norm2_g, w_in, q_norm_g, k_norm_g, rel_bias, w_pool, pool_scale, w_out, w_up, w_down):
    b, s_len, d = x_prompt.shape
    db, t, _ = x_sample.shape
    depth = w_ada.shape[0]
    assert depth == 1, "single-layer step"
    n_heads = rel_bias.shape[1]
    a = n_heads * HEAD_DIM
    wb = cache_k.shape[2]
    l = 0

    ada = _adaln(jnp.concatenate([c_prompt, c_sample], axis=0), w_ada[l], b_ada[l])
    mods = [ada[:, i * d:(i + 1) * d] for i in range(N_ADA)]
    mp = [m[:b] for m in mods]
    msm = [m[b:] for m in mods]

    w_in_b = w_in[l].astype(BF16)
    w_out_b = w_out[l].astype(BF16)
    w_up_b = w_up[l].astype(BF16)
    w_down_b = w_down[l].astype(BF16)
    w_pool_b = w_pool[l].astype(BF16)
    g1 = norm1_g[l].reshape(1, d)
    g2 = norm2_g[l].reshape(1, d)
    qg = jnp.tile(q_norm_g[l], n_heads).reshape(1, a)
    kg = jnp.tile(k_norm_g[l], n_heads).reshape(1, a)
    ps = pool_scale[l].reshape(1, -1)
    seg_i = jnp.arange(SEG_W) // HEAD_DIM
    seg = jnp.where(seg_i[:, None] == seg_i[None, :], 1.0 / HEAD_DIM, 0.0).astype(BF16)

    xp2 = x_prompt.reshape(b * s_len, d)
    q, k, v, u, kt, vt = _mixer_in(xp2, mp[0], mp[1], g1, w_in_b, qg, kg, seg, tm=512, rows_per_mod=s_len,
                                   feature_major=True)
    attn = _prompt_attention(q.reshape(b, s_len, a), k.reshape(b, s_len, a), v.reshape(b, s_len, a), rel_bias)
    u3 = u.reshape(b, s_len, -1)
    pool = _pool_mix(u3, jnp.zeros((1, POOL_CTX, u3.shape[2]), F32), w_pool_b, ps, seqs_per_step=1, pos0=0)
    keep = min(max(w for w, _ in DILATED_BRANCHES), s_len)
    win = lambda c: jnp.transpose(c.reshape(b, n_heads, HEAD_DIM, s_len), (0, 3, 1, 2))[None, :, s_len - keep:]
    k_win_prompt = win(kt)
    v_win_prompt = win(vt)
    pool_prompt = u3[None, :, s_len - POOL_CTX:]

    xs2 = x_sample.reshape(db * t, d)
    qs, ks, vs, us = _mixer_in(xs2, msm[0], msm[1], g1, w_in_b, qg, kg, seg, tm=256, rows_per_mod=t)
    to_fm = lambda c: jnp.transpose(c, (0, 2, 3, 1)).reshape(db, a, wb)
    from_fm = lambda c: jnp.transpose(c.reshape(db, n_heads, HEAD_DIM, wb), (0, 3, 1, 2))[None]
    y_prompt, attn_s, k_win, v_win = _tail_and_sample_attention(
        xp2, attn.reshape(b * s_len, a), pool.reshape(b * s_len, -1),
        mp[2], mp[3], mp[4], mp[5], g2, w_out_b, w_up_b, w_down_b,
        qs.reshape(db, t, a), ks.reshape(db, t, a), vs.reshape(db, t, a),
        to_fm(cache_k[l]), to_fm(cache_v[l]), rel_bias, tm=512, rows_per_mod=s_len)
    y_prompt = y_prompt.reshape(b, s_len, d)
    us3 = us.reshape(db, t, -1)
    pool_s = _pool_mix(us3, state_pool[l], w_pool_b, ps, seqs_per_step=db, pos0=PAST_LEN)
    y_sample = _layer_tail(xs2, attn_s.reshape(db * t, a), pool_s.reshape(db * t, -1),
                           msm[2], msm[3], msm[4], msm[5], g2, w_out_b, w_up_b, w_down_b,
                           tm=256, rows_per_mod=t).reshape(db, t, d)
    k_win_sample = from_fm(k_win)
    v_win_sample = from_fm(v_win)
    pool_sample = jnp.concatenate([state_pool[l], us3], axis=1)[None, :, -POOL_CTX:]

    return (y_prompt, y_sample, k_win_prompt, v_win_prompt, pool_prompt,
            k_win_sample, v_win_sample, pool_sample)
```

```python
import functools
import math

import jax
import jax.numpy as jnp
from jax import lax
from jax.experimental import pallas as pl
from jax.experimental.pallas import tpu as pltpu

F32 = jnp.float32
BF16 = jnp.bfloat16

HEAD_DIM = 64
DILATED_BRANCHES = ((128, 1), (512, 4), (2048, 16))
NUM_BUCKETS = 32
MAX_DISTANCE = 2048
POOL_WINDOWS = (2, 4, 8, 16)
PAST_LEN = 8192
POOL_CTX = max(POOL_WINDOWS) - 1
POOL_PAD = 16
POOL_LEAD = max(POOL_WINDOWS) // 2
N_ADA = 6
EPS = 1e-6
NEG_INF = -1e30
LOG2E = math.log2(math.e)

LANES = 128
SEG_W = 256
VMEM_LIMIT = 56 * 1024 * 1024
VMEM_LIMIT_BIG = 62 * 1024 * 1024


def _rel_bucket(dist):
    exact = NUM_BUCKETS // 2
    d = jnp.maximum(dist.astype(F32), 1.0)
    large = exact + (jnp.log(d / exact) / math.log(MAX_DISTANCE / exact)
                     * (NUM_BUCKETS - exact)).astype(jnp.int32)
    large = jnp.minimum(large, NUM_BUCKETS - 1)
    return jnp.where(dist < exact, dist, large)


def _bias_from_buckets(bkt, relb_ref, head):
    out = jnp.full(bkt.shape, NEG_INF, F32)
    for b in range(NUM_BUCKETS):
        out = jnp.where(bkt == b, relb_ref[b, head], out)
    return out


def _adaln_kernel(c_ref, w_ref, b_ref, o_ref):
    c = c_ref[...]
    s = c / (1.0 + jnp.exp(-c))
    o_ref[...] = jnp.dot(s, w_ref[...], preferred_element_type=F32,
                         precision=lax.Precision.HIGHEST) + b_ref[...]


def _adaln(c, w, b, tn=1024):
    m, d = c.shape
    n = w.shape[1]
    return pl.pallas_call(
        _adaln_kernel,
        out_shape=jax.ShapeDtypeStruct((m, n), F32),
        grid=(n // tn,),
        in_specs=[pl.BlockSpec((m, d), lambda j: (0, 0)),
                  pl.BlockSpec((d, tn), lambda j: (0, j)),
                  pl.BlockSpec((1, tn), lambda j: (0, j))],
        out_specs=pl.BlockSpec((m, tn), lambda j: (0, j)),
        compiler_params=pltpu.CompilerParams(dimension_semantics=("arbitrary",),
                                             vmem_limit_bytes=VMEM_LIMIT),
        name="adaln",
    )(c, w, b.reshape(1, n))


def _modulated_norm(x, g, shift, scale, groups):
    tm, d = x.shape
    ms = jnp.mean(x * x, axis=-1, keepdims=True)
    y = x * lax.rsqrt(ms + EPS) * g
    if groups == 1:
        return y * (1.0 + scale) + shift
    y3 = y.reshape(groups, tm // groups, d)
    return (y3 * (1.0 + scale[:, None, :]) + shift[:, None, :]).reshape(tm, d)


def _per_group(v, x, groups):
    tm, d = x.shape
    if groups == 1:
        return v * x
    return (v[:, None, :] * x.reshape(groups, tm // groups, d)).reshape(tm, d)


def _mixer_in_kernel(x_ref, shift_ref, scale_ref, g1_ref, w_ref, qg_ref, kg_ref, seg_ref,
                     q_ref, k_ref, v_ref, u_ref, *fm_refs, groups):
    a = q_ref.shape[1]
    h = _modulated_norm(x_ref[...], g1_ref[...], shift_ref[...], scale_ref[...], groups)
    hb = h.astype(BF16)

    def head_norm(z, g):
        parts = []
        for c in range(a // SEG_W):
            zc = z[:, c * SEG_W:(c + 1) * SEG_W]
            ms = jnp.dot((zc * zc).astype(BF16), seg_ref[...], preferred_element_type=F32)
            parts.append(zc * lax.rsqrt(ms + EPS))
        return jnp.concatenate(parts, axis=-1) * g

    zq = jnp.dot(hb, w_ref[:, 0:a], preferred_element_type=F32)
    q_ref[...] = head_norm(zq, qg_ref[...]) * (HEAD_DIM ** -0.5)
    zk = jnp.dot(hb, w_ref[:, a:2 * a], preferred_element_type=F32)
    k = head_norm(zk, kg_ref[...])
    k_ref[...] = k
    v = jnp.dot(hb, w_ref[:, 2 * a:3 * a], preferred_element_type=F32)
    v_ref[...] = v
    u_ref[...] = jnp.dot(hb, w_ref[:, 3 * a:], preferred_element_type=F32)
    if fm_refs:
        kt_ref, vt_ref = fm_refs
        kt_ref[...] = k.T
        vt_ref[...] = v.T


def _mixer_in(x2, shift, scale, g1, w_in_b, qg, kg, seg, *, tm, rows_per_mod, feature_major=False):
    rows, d = x2.shape
    d_in = w_in_b.shape[1]
    a = qg.shape[1]
    p = d_in - 3 * a
    if rows_per_mod >= tm:
        groups = 1
        per = rows_per_mod // tm
        shift = shift.reshape(-1, 1, d)
        scale = scale.reshape(-1, 1, d)
        mod_spec = pl.BlockSpec((None, 1, d), lambda i: (i // per, 0, 0))
    else:
        assert not feature_major
        groups = tm // rows_per_mod
        mod_spec = pl.BlockSpec((groups, d), lambda i: (i, 0))
    const = lambda i: (0, 0)
    row_spec = lambda w: pl.BlockSpec((tm, w), lambda i: (i, 0))
    out_shape = [jax.ShapeDtypeStruct((rows, a), F32)] * 3 + [jax.ShapeDtypeStruct((rows, p), F32)]
    out_specs = [row_spec(a)] * 3 + [row_spec(p)]
    if feature_major:
        out_shape += [jax.ShapeDtypeStruct((rows // rows_per_mod, a, rows_per_mod), F32)] * 2
        out_specs += [pl.BlockSpec((None, a, tm), lambda i: (i // per, 0, i % per))] * 2
    return pl.pallas_call(
        functools.partial(_mixer_in_kernel, groups=groups),
        out_shape=out_shape,
        grid=(rows // tm,),
        in_specs=[row_spec(d), mod_spec, mod_spec,
                  pl.BlockSpec((1, d), const),
                  pl.BlockSpec((d, d_in), const),
                  pl.BlockSpec((1, a), const), pl.BlockSpec((1, a), const),
                  pl.BlockSpec((SEG_W, SEG_W), const)],
        out_specs=out_specs,
        compiler_params=pltpu.CompilerParams(dimension_semantics=("arbitrary",),
                                             vmem_limit_bytes=VMEM_LIMIT),
        name="mixer_in",
    )(x2, shift, scale, g1, w_in_b, qg, kg, seg)


def _prompt_attn_kernel(relb_ref, bkt_ref, q_ref, k_ref, v_ref, o_ref,
                        qd, kd, vd, bias_s, acc_s, m_s, l_s, *, blk, group):
    s_len = q_ref.shape[0]
    nbr = len(DILATED_BRANCHES)
    mid = DILATED_BRANCHES[1][1]
    msub = s_len // mid
    pair = pl.program_id(0)
    head0 = lax.broadcasted_iota(jnp.int32, (1, LANES), 1) < HEAD_DIM

    @pl.when(pl.program_id(1) == 0)
    def _():
        for bi in range(nbr):
            for hh in range(2):
                bias_s[bi, hh * blk:(hh + 1) * blk, :] = LOG2E * _bias_from_buckets(bkt_ref[bi], relb_ref,
                                                                                      2 * pair + hh)

    for bi, (_, dil) in enumerate(DILATED_BRANCHES):
        sub = s_len // dil
        for src, dst, scale in ((q_ref, qd, LOG2E), (k_ref, kd, None), (v_ref, vd, None)):
            for r in range(dil):
                rows = src[...] if dil == 1 else src[pl.ds(r, sub, stride=dil), :]
                if scale is not None:
                    rows = rows * scale
                dst[bi, r * sub:(r + 1) * sub, :] = rows.astype(BF16)

    def unit(bi, base, dst, first):
        nk = blk if first else 2 * blk
        kbase = base if first else base - blk
        qb = qd[bi, pl.ds(base, blk), :]
        zero = jnp.zeros_like(qb)
        q2 = jnp.concatenate([jnp.where(head0, qb, zero), jnp.where(head0, zero, qb)], axis=0)
        s = lax.dot_general(q2, kd[bi, pl.ds(kbase, nk), :], (((1,), (1,)), ((), ())),
                            preferred_element_type=F32)
        s = s + (bias_s[bi, :, blk:] if first else bias_s[bi])
        m = jnp.max(s, axis=-1, keepdims=True)
        p = jnp.exp2(s - m)
        l = jnp.sum(p, axis=-1, keepdims=True)
        acc = jnp.dot(p.astype(BF16), vd[bi, pl.ds(kbase, nk), :], preferred_element_type=F32)
        full = (blk, LANES)
        acc_s[bi, dst, :] = jnp.where(head0, acc[:blk], acc[blk:])
        m_s[bi, dst, :] = jnp.where(head0, jnp.broadcast_to(m[:blk], full), jnp.broadcast_to(m[blk:], full))
        l_s[bi, dst, :] = jnp.where(head0, jnp.broadcast_to(l[:blk], full), jnp.broadcast_to(l[blk:], full))

    for bi, (_, dil) in enumerate(DILATED_BRANCHES):
        sub = s_len // dil
        nb = sub // blk
        if dil <= mid:
            def residue(r, carry, bi=bi, sub=sub, nb=nb):
                base = pl.multiple_of(r * sub, blk)
                unit(bi, base, pl.ds(base, blk), True)

                def later(n, c):
                    b2 = pl.multiple_of(r * sub + n * blk, blk)
                    unit(bi, b2, pl.ds(b2, blk), False)
                    return c

                lax.fori_loop(1, nb, later, 0, unroll=min(group, nb - 1))
                return carry

            lax.fori_loop(0, dil, residue, 0, unroll=max(1, min(dil, group // nb)))
        else:
            step = dil // mid
            for r in range(dil):
                for n in range(nb):
                    start = (r % mid) * msub + r // mid + n * blk * step
                    unit(bi, r * sub + n * blk, pl.ds(start, blk, stride=step), n == 0)

    for r in range(mid):
        for c in range(msub // blk):
            res = slice(r * msub + c * blk, r * msub + (c + 1) * blk)
            nat = pl.ds(c * blk * mid + r, blk, stride=mid)
            rows = [nat] + [res] * (nbr - 1)
            m = [m_s[bi, rows[bi], :] for bi in range(nbr)]
            top = functools.reduce(jnp.maximum, m)
            num = jnp.zeros((blk, LANES), F32)
            den = jnp.zeros((blk, LANES), F32)
            for bi in range(nbr):
                w = jnp.exp2(m[bi] - top)
                num = num + w * acc_s[bi, rows[bi], :]
                den = den + w * l_s[bi, rows[bi], :]
            o_ref[nat, :] = num / den


def _prompt_attention(q, k, v, rel_bias, blk=128, group=16):
    b, s_len, a = q.shape
    n_pairs = a // LANES
    nbr = len(DILATED_BRANCHES)
    assert all(w // d == blk and (s_len // d) % blk == 0 for w, d in DILATED_BRANCHES)
    qi = jnp.arange(blk)[:, None]
    ki = jnp.arange(2 * blk)[None, :]
    dist = qi + blk - ki
    ok = (dist >= 0) & (dist <= blk)
    bkt = jnp.stack([jnp.where(ok, _rel_bucket(jnp.clip(dist, 0, blk) * d), -1)
                     for _, d in DILATED_BRANCHES]).astype(jnp.int32)
    seq_spec = pl.BlockSpec((None, s_len, LANES), lambda p, i: (i, 0, p))
    return pl.pallas_call(
        functools.partial(_prompt_attn_kernel, blk=blk, group=group),
        out_shape=jax.ShapeDtypeStruct((b, s_len, a), F32),
        grid=(n_pairs, b),
        in_specs=[pl.BlockSpec(memory_space=pltpu.SMEM),
                  pl.BlockSpec((nbr, blk, 2 * blk), lambda p, i: (0, 0, 0)),
                  seq_spec, seq_spec, seq_spec],
        out_specs=seq_spec,
        scratch_shapes=[pltpu.VMEM((nbr, s_len, LANES), BF16)] * 3
                       + [pltpu.VMEM((nbr, 2 * blk, 2 * blk), F32)]
                       + [pltpu.VMEM((nbr, s_len, LANES), F32)] * 3,
        compiler_params=pltpu.CompilerParams(dimension_semantics=("arbitrary", "arbitrary"),
                                             vmem_limit_bytes=VMEM_LIMIT),
        name="prompt_attn",
    )(rel_bias, bkt, q, k, v)


def _sample_tables(t, wb, nk):
    dist = wb + jnp.arange(t)[:, None] - jnp.arange(nk)[None, :]
    mult = jnp.zeros((t, nk), jnp.int32)
    for w, d in DILATED_BRANCHES:
        mult = mult + ((dist >= 0) & (dist <= w) & (dist % d == 0)).astype(jnp.int32)
    bkt = jnp.where(mult > 0, _rel_bucket(jnp.maximum(dist, 0)), -1).astype(jnp.int32)
    return bkt, mult.astype(F32)


def _sample_slab(bias, mult, q_ref, kn_ref, vn_ref, ck_ref, cv_ref, o_ref, ko_ref, vo_ref, kall, vall,
                 row_chunk):
    t, slab = q_ref.shape
    wb = ck_ref.shape[1]
    nk = kall.shape[1]
    pad = nk - wb
    hps = slab // HEAD_DIM

    def append(new_ref, cache_ref, out_ref, all_ref):
        new_t = jnp.concatenate([new_ref[...], jnp.zeros((pad - t, slab), F32)], axis=0).T
        for c in range(slab // row_chunk):
            rows = slice(c * row_chunk, (c + 1) * row_chunk)
            ext = jnp.concatenate([cache_ref[rows, :], new_t[rows, :]], axis=1)
            all_ref[rows, :] = ext.astype(BF16)
            out_ref[rows, :] = pltpu.roll(ext, nk - t, axis=1)[:, 0:wb]

    append(kn_ref, ck_ref, ko_ref, kall)
    append(vn_ref, cv_ref, vo_ref, vall)

    lane = lax.broadcasted_iota(jnp.int32, (hps * t, slab), 1)
    row = lax.broadcasted_iota(jnp.int32, (hps * t, slab), 0)
    own = (lane // HEAD_DIM) == (row // t)
    qs = jnp.where(own, jnp.concatenate([q_ref[...]] * hps, axis=0), 0.0).astype(BF16)
    s = jnp.dot(qs, kall[...], preferred_element_type=F32) + bias
    m = jnp.max(s, axis=-1, keepdims=True)
    p = jnp.exp(s - m) * jnp.concatenate([mult] * hps, axis=0)
    l = jnp.sum(p, axis=-1, keepdims=True)
    o = lax.dot_general(p.astype(BF16), vall[...], (((1,), (1,)), ((), ())),
                        preferred_element_type=F32) / l
    o = jnp.where(own, o, 0.0)
    res = o[0:t]
    for hh in range(1, hps):
        res = res + o[hh * t:(hh + 1) * t]
    o_ref[...] = res


def _tail_sample_kernel(x_ref, attn_ref, pool_ref, gate1_ref, shift2_ref, scale2_ref, gate2_ref, g2_ref,
                        wout_ref, wup_ref, wdown_ref,
                        relb_ref, bkt_ref, mult_ref, q_ref, kn_ref, vn_ref, ck_ref, cv_ref,
                        y_ref, o_ref, ko_ref, vo_ref,
                        hb_s, acc_s, kall, vall, bias_s, *, n_slab, row_chunk, ff_chunk):
    i = pl.program_id(0)
    j = pl.program_id(1)
    n_phase = pl.num_programs(1)
    a = attn_ref.shape[1]
    t, slab = q_ref.shape
    hps = slab // HEAD_DIM

    @pl.when((i == 0) & (j == 0))
    def _():
        for h in range(n_slab * hps):
            bias_s[h // hps, (h % hps) * t:(h % hps + 1) * t, :] = _bias_from_buckets(bkt_ref[...], relb_ref, h)

    @pl.when(j == 0)
    def _():
        mix = jnp.dot(attn_ref[...].astype(BF16), wout_ref[0:a, :], preferred_element_type=F32)
        mix = mix + jnp.dot(pool_ref[...].astype(BF16), wout_ref[a:, :], preferred_element_type=F32)
        x1 = x_ref[...] + gate1_ref[...] * mix
        y_ref[...] = x1
        hb_s[...] = _modulated_norm(x1, g2_ref[...], shift2_ref[...], scale2_ref[...], 1).astype(BF16)
        acc_s[...] = jnp.zeros(acc_s.shape, F32)

    sl = (i * n_phase + j) % n_slab
    _sample_slab(bias_s[sl], mult_ref[...], q_ref, kn_ref, vn_ref, ck_ref, cv_ref,
                 o_ref, ko_ref, vo_ref, kall, vall, row_chunk)
    ff = pl.ds(pl.multiple_of(j * ff_chunk, ff_chunk), ff_chunk)
    act = jnp.maximum(jnp.dot(hb_s[...], wup_ref[:, ff], preferred_element_type=F32), 0.0)
    acc_s[...] += jnp.dot((act * act).astype(BF16), wdown_ref[ff, :], preferred_element_type=F32)

    @pl.when(j == n_phase - 1)
    def _():
        y_ref[...] = y_ref[...] + gate2_ref[...] * acc_s[...]


def _tail_and_sample_attention(x2, attn2, pool2, gate1, shift2, scale2, gate2, g2, w_out_b, w_up_b, w_down_b,
                               q, k_new, v_new, cache_kt, cache_vt, rel_bias,
                               *, tm, rows_per_mod, ff_chunk=1024, slab=256, pad=128, row_chunk=64):
    rows, d = x2.shape
    a = attn2.shape[1]
    p = pool2.shape[1]
    d_ff = w_up_b.shape[1]
    db, t, _ = q.shape
    wb = cache_kt.shape[2]
    nk = wb + pad
    n_slab = a // slab
    n_phase = d_ff // ff_chunk
    n_tiles = rows // tm
    hps = slab // HEAD_DIM
    assert pad >= t and wb >= max(w for w, _ in DILATED_BRANCHES)
    assert n_tiles * n_phase == db * n_slab and rows_per_mod % tm == 0
    per = rows_per_mod // tm
    bkt, mult = _sample_tables(t, wb, nk)
    mods = [m.reshape(-1, 1, d) for m in (gate1, shift2, scale2, gate2)]

    row_spec = lambda w: pl.BlockSpec((tm, w), lambda i, j: (i, 0))
    mod_spec = pl.BlockSpec((None, 1, d), lambda i, j: (i // per, 0, 0))
    resident = lambda shape: pl.BlockSpec(shape, lambda i, j: (0,) * len(shape), pipeline_mode=pl.Buffered(1))
    seq = lambda i, j: (i * n_phase + j) // n_slab
    sl = lambda i, j: (i * n_phase + j) % n_slab
    new_spec = pl.BlockSpec((None, t, slab), lambda i, j: (seq(i, j), 0, sl(i, j)))
    win_spec = pl.BlockSpec((None, slab, wb), lambda i, j: (seq(i, j), sl(i, j), 0))
    tab_spec = pl.BlockSpec((t, nk), lambda i, j: (0, 0))
    return pl.pallas_call(
        functools.partial(_tail_sample_kernel, n_slab=n_slab, row_chunk=row_chunk, ff_chunk=ff_chunk),
        out_shape=[jax.ShapeDtypeStruct((rows, d), F32),
                   jax.ShapeDtypeStruct((db, t, a), F32),
                   jax.ShapeDtypeStruct((db, a, wb), F32),
                   jax.ShapeDtypeStruct((db, a, wb), F32)],
        grid=(n_tiles, n_phase),
        in_specs=[row_spec(d), row_spec(a), row_spec(p), mod_spec, mod_spec, mod_spec, mod_spec,
                  pl.BlockSpec((1, d), lambda i, j: (0, 0)),
                  resident((a + p, d)), resident((d, d_ff)), resident((d_ff, d)),
                  pl.BlockSpec(memory_space=pltpu.SMEM), tab_spec, tab_spec,
                  new_spec, new_spec, new_spec, win_spec, win_spec],
        out_specs=[row_spec(d), new_spec, win_spec, win_spec],
        scratch_shapes=[pltpu.VMEM((tm, d), BF16), pltpu.VMEM((tm, d), F32),
                        pltpu.VMEM((slab, nk), BF16), pltpu.VMEM((slab, nk), BF16),
                        pltpu.VMEM((n_slab, hps * t, nk), F32)],
        compiler_params=pltpu.CompilerParams(dimension_semantics=("arbitrary", "arbitrary"),
                                             vmem_limit_bytes=VMEM_LIMIT_BIG),
        name="tail_sample",
    )(x2, attn2, pool2, *mods, g2, w_out_b, w_up_b, w_down_b,
      rel_bias, bkt, mult, q, k_new, v_new, cache_kt, cache_vt)


def _pool_kernel(u_ref, ctx_ref, wp_ref, ps_ref, o_ref, buf_a, buf_b, *, pos0):
    nb, t, p = u_ref.shape
    grp = p // len(POOL_WINDOWS)
    n = POOL_PAD + t
    ctx0 = POOL_LEAD + POOL_PAD - POOL_CTX
    buf_a[:, 0:ctx0, :] = jnp.zeros((nb, ctx0, p), F32)
    buf_b[:, 0:POOL_LEAD, :] = jnp.zeros((nb, POOL_LEAD, p), F32)
    buf_a[:, ctx0:ctx0 + POOL_CTX, :] = jnp.broadcast_to(ctx_ref[...], (nb, POOL_CTX, p))
    buf_a[:, ctx0 + POOL_CTX:, :] = u_ref[...]

    src, dst = buf_a, buf_b
    where = []
    for g, w in enumerate(POOL_WINDOWS):
        assert w == 2 ** (g + 1)
        cols = slice(g * grp, p)
        dst[:, POOL_LEAD:POOL_LEAD + n, cols] = (src[:, POOL_LEAD:POOL_LEAD + n, cols]
                                                  + src[:, POOL_LEAD - w // 2:POOL_LEAD - w // 2 + n, cols])
        where.append(dst)
        src, dst = dst, src

    pos = (pos0 + lax.broadcasted_iota(jnp.int32, (1, t, 1), 1)).astype(F32)
    new0 = POOL_LEAD + POOL_PAD
    for g, w in enumerate(POOL_WINDOWS):
        cols = slice(g * grp, (g + 1) * grp)
        win = where[g][:, new0:new0 + t, cols]
        cnt = jnp.minimum(float(w), pos + 1.0)
        pooled = (win / cnt - u_ref[:, :, cols]).reshape(nb * t, grp)
        y = jnp.dot(pooled.astype(BF16), wp_ref[g], preferred_element_type=F32)
        o_ref[:, :, cols] = (y * ps_ref[:, cols]).reshape(nb, t, grp).astype(o_ref.dtype)


def _pool_mix(u, ctx, w_pool_b, pool_scale, *, seqs_per_step, pos0):
    nb, t, p = u.shape
    g = len(POOL_WINDOWS)
    bb = seqs_per_step
    u_spec = pl.BlockSpec((bb, t, p), lambda i: (i, 0, 0))
    if ctx.shape[0] == 1:
        ctx_spec = pl.BlockSpec((1, POOL_CTX, p), lambda i: (0, 0, 0))
    else:
        ctx_spec = pl.BlockSpec((bb, POOL_CTX, p), lambda i: (i, 0, 0))
    return pl.pallas_call(
        functools.partial(_pool_kernel, pos0=pos0),
        out_shape=jax.ShapeDtypeStruct((nb, t, p), BF16),
        grid=(nb // bb,),
        in_specs=[u_spec, ctx_spec,
                  pl.BlockSpec((g, p // g, p // g), lambda i: (0, 0, 0)),
                  pl.BlockSpec((1, p), lambda i: (0, 0))],
        out_specs=u_spec,
        scratch_shapes=[pltpu.VMEM((bb, POOL_LEAD + POOL_PAD + t, p), F32)] * 2,
        compiler_params=pltpu.CompilerParams(dimension_semantics=("arbitrary",),
                                             vmem_limit_bytes=VMEM_LIMIT),
        name="pool_mix",
    )(u, ctx, w_pool_b, pool_scale)


def _tail_kernel(x_ref, attn_ref, pool_ref, gate1_ref, shift2_ref, scale2_ref, gate2_ref, g2_ref,
                 wout_ref, wup_ref, wdown_ref, y_ref, *, groups, ff_chunk):
    a = attn_ref.shape[1]
    d_ff = wup_ref.shape[1]
    mix = jnp.dot(attn_ref[...].astype(BF16), wout_ref[0:a, :], preferred_element_type=F32)
    mix = mix + jnp.dot(pool_ref[...].astype(BF16), wout_ref[a:, :], preferred_element_type=F32)
    x1 = x_ref[...] + _per_group(gate1_ref[...], mix, groups)
    hb = _modulated_norm(x1, g2_ref[...], shift2_ref[...], scale2_ref[...], groups).astype(BF16)
    f = jnp.zeros(x1.shape, F32)
    for c in range(d_ff // ff_chunk):
        cols = slice(c * ff_chunk, (c + 1) * ff_chunk)
        act = jnp.maximum(jnp.dot(hb, wup_ref[:, cols], preferred_element_type=F32), 0.0)
        f = f + jnp.dot((act * act).astype(BF16), wdown_ref[cols, :], preferred_element_type=F32)
    y_ref[...] = x1 + _per_group(gate2_ref[...], f, groups)


def _layer_tail(x2, attn2, pool2, gate1, shift2, scale2, gate2, g2, w_out_b, w_up_b, w_down_b,
                *, tm, rows_per_mod, ff_chunk=1024):
    rows, d = x2.shape
    a = attn2.shape[1]
    p = pool2.shape[1]
    d_ff = w_up_b.shape[1]
    mods = [gate1, shift2, scale2, gate2]
    if rows_per_mod >= tm:
        groups = 1
        per = rows_per_mod // tm
        mods = [m.reshape(-1, 1, d) for m in mods]
        mod_spec = pl.BlockSpec((None, 1, d), lambda i: (i // per, 0, 0))
    else:
        groups = tm // rows_per_mod
        mod_spec = pl.BlockSpec((groups, d), lambda i: (i, 0))
    const = lambda i: (0, 0)
    row_spec = lambda w: pl.BlockSpec((tm, w), lambda i: (i, 0))
    resident = lambda shape: pl.BlockSpec(shape, const, pipeline_mode=pl.Buffered(1))
    return pl.pallas_call(
        functools.partial(_tail_kernel, groups=groups, ff_chunk=ff_chunk),
        out_shape=jax.ShapeDtypeStruct((rows, d), F32),
        grid=(rows // tm,),
        in_specs=[row_spec(d), row_spec(a), row_spec(p), mod_spec, mod_spec, mod_spec, mod_spec,
                  pl.BlockSpec((1, d), const),
                  resident((a + p, d)), resident((d, d_ff)), resident((d_ff, d))],
        out_specs=row_spec(d),
        compiler_params=pltpu.CompilerParams(dimension_semantics=("arbitrary",),
                                             vmem_limit_bytes=VMEM_LIMIT),
        name="layer_tail",
    )(x2, attn2, pool2, *mods, g2, w_out_b, w_up_b, w_down_b)


def kernel(x_prompt, x_sample, c_prompt, c_sample, cache_k, cache_v, state_pool, w_ada, b_ada, norm1_g, norm2_g, w_in, q_norm_g, k_norm_g, rel_bias, w_pool, pool_scale, w_out, w_up, w_down):
    b, s_len, d = x_prompt.shape
    db, t, _ = x_sample.shape
    depth = w_ada.shape[0]
    assert depth == 1, "single-layer step"
    n_heads = rel_bias.shape[1]
    a = n_heads * HEAD_DIM
    wb = cache_k.shape[2]
    l = 0

    ada = _adaln(jnp.concatenate([c_prompt, c_sample], axis=0), w_ada[l], b_ada[l])
    mods = [ada[:, i * d:(i + 1) * d] for i in range(N_ADA)]
    mp = [m[:b] for m in mods]
    msm = [m[b:] for m in mods]

    w_in_b = w_in[l].astype(BF16)
    w_out_b = w_out[l].astype(BF16)
    w_up_b = w_up[l].astype(BF16)
    w_down_b = w_down[l].astype(BF16)
    w_pool_b = w_pool[l].astype(BF16)
    g1 = norm1_g[l].reshape(1, d)
    g2 = norm2_g[l].reshape(1, d)
    qg = jnp.tile(q_norm_g[l], n_heads).reshape(1, a)
    kg = jnp.tile(k_norm_g[l], n_heads).reshape(1, a)
    ps = pool_scale[l].reshape(1, -1)
    seg_i = jnp.arange(SEG_W) // HEAD_DIM
    seg = jnp.where(seg_i[:, None] == seg_i[None, :], 1.0 / HEAD_DIM, 0.0).astype(BF16)

    xp2 = x_prompt.reshape(b * s_len, d)
    q, k, v, u, kt, vt = _mixer_in(xp2, mp[0], mp[1], g1, w_in_b, qg, kg, seg, tm=512, rows_per_mod=s_len,
                                   feature_major=True)
    attn = _prompt_attention(q.reshape(b, s_len, a), k.reshape(b, s_len, a), v.reshape(b, s_len, a), rel_bias)
    u3 = u.reshape(b, s_len, -1)
    pool = _pool_mix(u3, jnp.zeros((1, POOL_CTX, u3.shape[2]), F32), w_pool_b, ps, seqs_per_step=1, pos0=0)
    keep = min(max(w for w, _ in DILATED_BRANCHES), s_len)
    win = lambda c: jnp.transpose(c.reshape(b, n_heads, HEAD_DIM, s_len), (0, 3, 1, 2))[None, :, s_len - keep:]
    k_win_prompt = win(kt)
    v_win_prompt = win(vt)
    pool_prompt = u3[None, :, s_len - POOL_CTX:]

    xs2 = x_sample.reshape(db * t, d)
    qs, ks, vs, us = _mixer_in(xs2, msm[0], msm[1], g1, w_in_b, qg, kg, seg, tm=256, rows_per_mod=t)
    to_fm = lambda c: jnp.transpose(c, (0, 2, 3, 1)).reshape(db, a, wb)
    from_fm = lambda c: jnp.transpose(c.reshape(db, n_heads, HEAD_DIM, wb), (0, 3, 1, 2))[None]
    y_prompt, attn_s, k_win, v_win = _tail_and_sample_attention(
        xp2, attn.reshape(b * s_len, a), pool.reshape(b * s_len, -1),
        mp[2], mp[3], mp[4], mp[5], g2, w_out_b, w_up_b, w_down_b,
        qs.reshape(db, t, a), ks.reshape(db, t, a), vs.reshape(db, t, a),
        to_fm(cache_k[l]), to_fm(cache_v[l]), rel_bias, tm=512, rows_per_mod=s_len)
    y_prompt = y_prompt.reshape(b, s_len, d)
    us3 = us.reshape(db, t, -1)
    pool_s = _pool_mix(us3, state_pool[l], w_pool_b, ps, seqs_per_step=db, pos0=PAST_LEN)
    y_sample = _layer_tail(xs2, attn_s.reshape(db * t, a), pool_s.reshape(db * t, -1),
                           msm[2], msm[3], msm[4], msm[5], g2, w_out_b, w_up_b, w_down_b,
                           tm=256, rows_per_mod=t).reshape(db, t, d)
    k_win_sample = from_fm(k_win)
    v_win_sample = from_fm(v_win)
    pool_sample = jnp.concatenate([state_pool[l], us3], axis=1)[None, :, -POOL_CTX:]

    return (y_prompt, y_sample, k_win_prompt, v_win_prompt, pool_prompt,
            k_win_sample, v_win_sample, pool_sample)
```

```python
import functools
import math

import jax
import jax.numpy as jnp
from jax import lax
from jax.experimental import pallas as pl
from jax.experimental.pallas import tpu as pltpu

F32 = jnp.float32
BF16 = jnp.bfloat16

HEAD_DIM = 64
DILATED_BRANCHES = ((128, 1), (512, 4), (2048, 16))
NUM_BUCKETS = 32
MAX_DISTANCE = 2048
POOL_WINDOWS = (2, 4, 8, 16)
PAST_LEN = 8192
POOL_CTX = max(POOL_WINDOWS) - 1
POOL_PAD = 16
POOL_LEAD = max(POOL_WINDOWS) // 2
N_ADA = 6
EPS = 1e-6
NEG_INF = -1e30
LOG2E = math.log2(math.e)

LANES = 128
SEG_W = 256
VMEM_LIMIT = 56 * 1024 * 1024
VMEM_LIMIT_BIG = 62 * 1024 * 1024
RING_DEPTH = 3


def _rel_bucket(dist):
    exact = NUM_BUCKETS // 2
    d = jnp.maximum(dist.astype(F32), 1.0)
    large = exact + (jnp.log(d / exact) / math.log(MAX_DISTANCE / exact)
                     * (NUM_BUCKETS - exact)).astype(jnp.int32)
    large = jnp.minimum(large, NUM_BUCKETS - 1)
    return jnp.where(dist < exact, dist, large)


def _bias_from_buckets(bkt, relb_ref, head):
    out = jnp.full(bkt.shape, NEG_INF, F32)
    for b in range(NUM_BUCKETS):
        out = jnp.where(bkt == b, relb_ref[b, head], out)
    return out


def _adaln_kernel(c_ref, w_ref, b_ref, o_ref):
    c = c_ref[...]
    s = c / (1.0 + jnp.exp(-c))
    o_ref[...] = jnp.dot(s, w_ref[...], preferred_element_type=F32,
                         precision=lax.Precision.HIGHEST) + b_ref[...]


def _adaln(c, w, b, tn=1024):
    m, d = c.shape
    n = w.shape[1]
    return pl.pallas_call(
        _adaln_kernel,
        out_shape=jax.ShapeDtypeStruct((m, n), F32),
        grid=(n // tn,),
        in_specs=[pl.BlockSpec((m, d), lambda j: (0, 0)),
                  pl.BlockSpec((d, tn), lambda j: (0, j)),
                  pl.BlockSpec((1, tn), lambda j: (0, j))],
        out_specs=pl.BlockSpec((m, tn), lambda j: (0, j)),
        compiler_params=pltpu.CompilerParams(dimension_semantics=("arbitrary",),
                                             vmem_limit_bytes=VMEM_LIMIT),
        name="adaln",
    )(c, w, b.reshape(1, n))


def _modulated_norm(x, g, shift, scale, groups):
    tm, d = x.shape
    ms = jnp.mean(x * x, axis=-1, keepdims=True)
    y = x * lax.rsqrt(ms + EPS) * g
    if groups == 1:
        return y * (1.0 + scale) + shift
    y3 = y.reshape(groups, tm // groups, d)
    return (y3 * (1.0 + scale[:, None, :]) + shift[:, None, :]).reshape(tm, d)


def _per_group(v, x, groups):
    tm, d = x.shape
    if groups == 1:
        return v * x
    return (v[:, None, :] * x.reshape(groups, tm // groups, d)).reshape(tm, d)


def _mixer_in_kernel(x_ref, shift_ref, scale_ref, g1_ref, w_ref, qg_ref, kg_ref, seg_ref,
                     q_ref, k_ref, v_ref, u_ref, *fm_refs, groups):
    a = q_ref.shape[1]
    h = _modulated_norm(x_ref[...], g1_ref[...], shift_ref[...], scale_ref[...], groups)
    hb = h.astype(BF16)

    def head_norm(z, g):
        parts = []
        for c in range(a // SEG_W):
            zc = z[:, c * SEG_W:(c + 1) * SEG_W]
            ms = jnp.dot((zc * zc).astype(BF16), seg_ref[...], preferred_element_type=F32)
            parts.append(zc * lax.rsqrt(ms + EPS))
        return jnp.concatenate(parts, axis=-1) * g

    zq = jnp.dot(hb, w_ref[:, 0:a], preferred_element_type=F32)
    q_ref[...] = head_norm(zq, qg_ref[...]) * (HEAD_DIM ** -0.5)
    zk = jnp.dot(hb, w_ref[:, a:2 * a], preferred_element_type=F32)
    k = head_norm(zk, kg_ref[...])
    k_ref[...] = k
    v = jnp.dot(hb, w_ref[:, 2 * a:3 * a], preferred_element_type=F32)
    v_ref[...] = v
    u_ref[...] = jnp.dot(hb, w_ref[:, 3 * a:], preferred_element_type=F32)
    if fm_refs:
        kt_ref, vt_ref = fm_refs
        kt_ref[...] = k.T
        vt_ref[...] = v.T


def _mixer_in(x2, shift, scale, g1, w_in_b, qg, kg, seg, *, tm, rows_per_mod, feature_major=False):
    rows, d = x2.shape
    d_in = w_in_b.shape[1]
    a = qg.shape[1]
    p = d_in - 3 * a
    if rows_per_mod >= tm:
        groups = 1
        per = rows_per_mod // tm
        shift = shift.reshape(-1, 1, d)
        scale = scale.reshape(-1, 1, d)
        mod_spec = pl.BlockSpec((None, 1, d), lambda i: (i // per, 0, 0))
    else:
        assert not feature_major
        groups = tm // rows_per_mod
        mod_spec = pl.BlockSpec((groups, d), lambda i: (i, 0))
    const = lambda i: (0, 0)
    row_spec = lambda w: pl.BlockSpec((tm, w), lambda i: (i, 0))
    out_shape = [jax.ShapeDtypeStruct((rows, a), F32)] * 3 + [jax.ShapeDtypeStruct((rows, p), F32)]
    out_specs = [row_spec(a)] * 3 + [row_spec(p)]
    if feature_major:
        out_shape += [jax.ShapeDtypeStruct((rows // rows_per_mod, a, rows_per_mod), F32)] * 2
        out_specs += [pl.BlockSpec((None, a, tm), lambda i: (i // per, 0, i % per))] * 2
    return pl.pallas_call(
        functools.partial(_mixer_in_kernel, groups=groups),
        out_shape=out_shape,
        grid=(rows // tm,),
        in_specs=[row_spec(d), mod_spec, mod_spec,
                  pl.BlockSpec((1, d), const),
                  pl.BlockSpec((d, d_in), const),
                  pl.BlockSpec((1, a), const), pl.BlockSpec((1, a), const),
                  pl.BlockSpec((SEG_W, SEG_W), const)],
        out_specs=out_specs,
        compiler_params=pltpu.CompilerParams(dimension_semantics=("arbitrary",),
                                             vmem_limit_bytes=VMEM_LIMIT),
        name="mixer_in",
    )(x2, shift, scale, g1, w_in_b, qg, kg, seg)


def _prompt_attn_kernel(relb_ref, bkt_ref, q_ref, k_ref, v_ref, o_ref,
                        qd, kd, vd, bias_s, acc_s, m_s, l_s, *, blk, group):
    s_len = q_ref.shape[0]
    nbr = len(DILATED_BRANCHES)
    mid = DILATED_BRANCHES[1][1]
    msub = s_len // mid
    pair = pl.program_id(0)
    head0 = lax.broadcasted_iota(jnp.int32, (1, LANES), 1) < HEAD_DIM

    @pl.when(pl.program_id(1) == 0)
    def _():
        for bi in range(nbr):
            for hh in range(2):
                bias_s[bi, hh * blk:(hh + 1) * blk, :] = LOG2E * _bias_from_buckets(bkt_ref[bi], relb_ref,
                                                                                      2 * pair + hh)

    for bi, (_, dil) in enumerate(DILATED_BRANCHES):
        sub = s_len // dil
        for src, dst, scale in ((q_ref, qd, LOG2E), (k_ref, kd, None), (v_ref, vd, None)):
            for r in range(dil):
                rows = src[...] if dil == 1 else src[pl.ds(r, sub, stride=dil), :]
                if scale is not None:
                    rows = rows * scale
                dst[bi, r * sub:(r + 1) * sub, :] = rows.astype(BF16)

    def unit(bi, base, dst, first):
        nk = blk if first else 2 * blk
        kbase = base if first else base - blk
        qb = qd[bi, pl.ds(base, blk), :]
        zero = jnp.zeros_like(qb)
        q2 = jnp.concatenate([jnp.where(head0, qb, zero), jnp.where(head0, zero, qb)], axis=0)
        s = lax.dot_general(q2, kd[bi, pl.ds(kbase, nk), :], (((1,), (1,)), ((), ())),
                            preferred_element_type=F32)
        s = s + (bias_s[bi, :, blk:] if first else bias_s[bi])
        m = jnp.max(s, axis=-1, keepdims=True)
        p = jnp.exp2(s - m)
        l = jnp.sum(p, axis=-1, keepdims=True)
        acc = jnp.dot(p.astype(BF16), vd[bi, pl.ds(kbase, nk), :], preferred_element_type=F32)
        full = (blk, LANES)
        acc_s[bi, dst, :] = jnp.where(head0, acc[:blk], acc[blk:])
        m_s[bi, dst, :] = jnp.where(head0, jnp.broadcast_to(m[:blk], full), jnp.broadcast_to(m[blk:], full))
        l_s[bi, dst, :] = jnp.where(head0, jnp.broadcast_to(l[:blk], full), jnp.broadcast_to(l[blk:], full))

    for bi, (_, dil) in enumerate(DILATED_BRANCHES):
        sub = s_len // dil
        nb = sub // blk
        if dil <= mid:
            def residue(r, carry, bi=bi, sub=sub, nb=nb):
                base = pl.multiple_of(r * sub, blk)
                unit(bi, base, pl.ds(base, blk), True)

                def later(n, c):
                    b2 = pl.multiple_of(r * sub + n * blk, blk)
                    unit(bi, b2, pl.ds(b2, blk), False)
                    return c

                lax.fori_loop(1, nb, later, 0, unroll=min(group, nb - 1))
                return carry

            lax.fori_loop(0, dil, residue, 0, unroll=max(1, min(dil, group // nb)))
        else:
            step = dil // mid
            for r in range(dil):
                for n in range(nb):
                    start = (r % mid) * msub + r // mid + n * blk * step
                    unit(bi, r * sub + n * blk, pl.ds(start, blk, stride=step), n == 0)

    for r in range(mid):
        for c in range(msub // blk):
            res = slice(r * msub + c * blk, r * msub + (c + 1) * blk)
            nat = pl.ds(c * blk * mid + r, blk, stride=mid)
            rows = [nat] + [res] * (nbr - 1)
            m = [m_s[bi, rows[bi], :] for bi in range(nbr)]
            top = functools.reduce(jnp.maximum, m)
            num = jnp.zeros((blk, LANES), F32)
            den = jnp.zeros((blk, LANES), F32)
            for bi in range(nbr):
                w = jnp.exp2(m[bi] - top)
                num = num + w * acc_s[bi, rows[bi], :]
                den = den + w * l_s[bi, rows[bi], :]
            o_ref[nat, :] = num / den


def _prompt_attention(q, k, v, rel_bias, blk=128, group=16):
    b, s_len, a = q.shape
    n_pairs = a // LANES
    nbr = len(DILATED_BRANCHES)
    assert all(w // d == blk and (s_len // d) % blk == 0 for w, d in DILATED_BRANCHES)
    qi = jnp.arange(blk)[:, None]
    ki = jnp.arange(2 * blk)[None, :]
    dist = qi + blk - ki
    ok = (dist >= 0) & (dist <= blk)
    bkt = jnp.stack([jnp.where(ok, _rel_bucket(jnp.clip(dist, 0, blk) * d), -1)
                     for _, d in DILATED_BRANCHES]).astype(jnp.int32)
    seq_spec = pl.BlockSpec((None, s_len, LANES), lambda p, i: (i, 0, p))
    return pl.pallas_call(
        functools.partial(_prompt_attn_kernel, blk=blk, group=group),
        out_shape=jax.ShapeDtypeStruct((b, s_len, a), F32),
        grid=(n_pairs, b),
        in_specs=[pl.BlockSpec(memory_space=pltpu.SMEM),
                  pl.BlockSpec((nbr, blk, 2 * blk), lambda p, i: (0, 0, 0)),
                  seq_spec, seq_spec, seq_spec],
        out_specs=seq_spec,
        scratch_shapes=[pltpu.VMEM((nbr, s_len, LANES), BF16)] * 3
                       + [pltpu.VMEM((nbr, 2 * blk, 2 * blk), F32)]
                       + [pltpu.VMEM((nbr, s_len, LANES), F32)] * 3,
        compiler_params=pltpu.CompilerParams(dimension_semantics=("arbitrary", "arbitrary"),
                                             vmem_limit_bytes=VMEM_LIMIT),
        name="prompt_attn",
    )(rel_bias, bkt, q, k, v)


def _sample_tables(t, wb, nk):
    dist = wb + jnp.arange(t)[:, None] - jnp.arange(nk)[None, :]
    mult = jnp.zeros((t, nk), jnp.int32)
    for w, d in DILATED_BRANCHES:
        mult = mult + ((dist >= 0) & (dist <= w) & (dist % d == 0)).astype(jnp.int32)
    bkt = jnp.where(mult > 0, _rel_bucket(jnp.maximum(dist, 0)), -1).astype(jnp.int32)
    return bkt, mult.astype(F32)


def _sample_slab(bias, mult, q_ref, kn_ref, vn_ref, ck_ref, cv_ref, o_ref, ko_ref, vo_ref, kall, vall,
                 row_chunk):
    t, slab = q_ref.shape
    wb = ck_ref.shape[1]
    nk = kall.shape[1]
    pad = nk - wb
    hps = slab // HEAD_DIM

    def append(new_ref, cache_ref, out_ref, all_ref):
        new_t = jnp.concatenate([new_ref[...], jnp.zeros((pad - t, slab), F32)], axis=0).T
        for c in range(slab // row_chunk):
            rows = slice(c * row_chunk, (c + 1) * row_chunk)
            ext = jnp.concatenate([cache_ref[rows, :], new_t[rows, :]], axis=1)
            all_ref[rows, :] = ext.astype(BF16)
            out_ref[rows, :] = pltpu.roll(ext, nk - t, axis=1)[:, 0:wb]

    append(kn_ref, ck_ref, ko_ref, kall)
    append(vn_ref, cv_ref, vo_ref, vall)

    lane = lax.broadcasted_iota(jnp.int32, (hps * t, slab), 1)
    row = lax.broadcasted_iota(jnp.int32, (hps * t, slab), 0)
    own = (lane // HEAD_DIM) == (row // t)
    qs = jnp.where(own, jnp.concatenate([q_ref[...]] * hps, axis=0), 0.0).astype(BF16)
    s = jnp.dot(qs, kall[...], preferred_element_type=F32) + bias
    m = jnp.max(s, axis=-1, keepdims=True)
    p = jnp.exp(s - m) * jnp.concatenate([mult] * hps, axis=0)
    l = jnp.sum(p, axis=-1, keepdims=True)
    o = lax.dot_general(p.astype(BF16), vall[...], (((1,), (1,)), ((), ())),
                        preferred_element_type=F32) / l
    o = jnp.where(own, o, 0.0)
    res = o[0:t]
    for hh in range(1, hps):
        res = res + o[hh * t:(hh + 1) * t]
    o_ref[...] = res


def _tail_sample_kernel(x_ref, attn_ref, pool_ref, gate1_ref, shift2_ref, scale2_ref, gate2_ref, g2_ref,
                        wout_ref, wup_ref, wdown_ref,
                        relb_ref, bkt_ref, mult_ref, q_ref, kn_ref, vn_ref, ck_ref, cv_ref,
                        y_ref, o_ref, ko_ref, vo_ref,
                        hb_s, acc_s, kall, vall, bias_s, ring_k, ring_v, ring_sem,
                        *, n_slab, row_chunk, ff_chunk):
    i = pl.program_id(0)
    j = pl.program_id(1)
    n_phase = pl.num_programs(1)
    a = attn_ref.shape[1]
    t, slab = q_ref.shape
    hps = slab // HEAD_DIM
    step = i * n_phase + j
    n_steps = pl.num_programs(0) * n_phase

    def window_copies(unit, slot):
        seq = unit // n_slab
        rows = pl.ds(pl.multiple_of((unit % n_slab) * slab, slab), slab)
        return [pltpu.make_async_copy(src.at[seq, rows, :], ring.at[slot], ring_sem.at[slot, which])
                for which, (src, ring) in enumerate(((ck_ref, ring_k), (cv_ref, ring_v)))]

    @pl.when(step == 0)
    def _():
        for u in range(RING_DEPTH - 1):
            for c in window_copies(u, u):
                c.start()

    ahead = step + (RING_DEPTH - 1)

    @pl.when(ahead < n_steps)
    def _():
        for c in window_copies(ahead, ahead % RING_DEPTH):
            c.start()

    @pl.when((i == 0) & (j == 0))
    def _():
        for h in range(n_slab * hps):
            bias_s[h // hps, (h % hps) * t:(h % hps + 1) * t, :] = _bias_from_buckets(bkt_ref[...], relb_ref, h)

    @pl.when(j == 0)
    def _():
        mix = jnp.dot(attn_ref[...].astype(BF16), wout_ref[0:a, :], preferred_element_type=F32)
        mix = mix + jnp.dot(pool_ref[...].astype(BF16), wout_ref[a:, :], preferred_element_type=F32)
        x1 = x_ref[...] + gate1_ref[...] * mix
        y_ref[...] = x1
        hb_s[...] = _modulated_norm(x1, g2_ref[...], shift2_ref[...], scale2_ref[...], 1).astype(BF16)
        acc_s[...] = jnp.zeros(acc_s.shape, F32)

    slot = step % RING_DEPTH
    for c in window_copies(step, slot):
        c.wait()
    _sample_slab(bias_s[step % n_slab], mult_ref[...], q_ref, kn_ref, vn_ref, ring_k.at[slot], ring_v.at[slot],
                 o_ref, ko_ref, vo_ref, kall, vall, row_chunk)
    ff = pl.ds(pl.multiple_of(j * ff_chunk, ff_chunk), ff_chunk)
    act = jnp.maximum(jnp.dot(hb_s[...], wup_ref[:, ff], preferred_element_type=F32), 0.0)
    acc_s[...] += jnp.dot((act * act).astype(BF16), wdown_ref[ff, :], preferred_element_type=F32)

    @pl.when(j == n_phase - 1)
    def _():
        y_ref[...] = y_ref[...] + gate2_ref[...] * acc_s[...]


def _tail_and_sample_attention(x2, attn2, pool2, gate1, shift2, scale2, gate2, g2, w_out_b, w_up_b, w_down_b,
                               q, k_new, v_new, cache_kt, cache_vt, rel_bias,
                               *, tm, rows_per_mod, ff_chunk=1024, slab=256, pad=128, row_chunk=64):
    rows, d = x2.shape
    a = attn2.shape[1]
    p = pool2.shape[1]
    d_ff = w_up_b.shape[1]
    db, t, _ = q.shape
    wb = cache_kt.shape[2]
    nk = wb + pad
    n_slab = a // slab
    n_phase = d_ff // ff_chunk
    n_tiles = rows // tm
    hps = slab // HEAD_DIM
    assert pad >= t and wb >= max(w for w, _ in DILATED_BRANCHES)
    assert n_tiles * n_phase == db * n_slab and rows_per_mod % tm == 0
    per = rows_per_mod // tm
    bkt, mult = _sample_tables(t, wb, nk)
    mods = [m.reshape(-1, 1, d) for m in (gate1, shift2, scale2, gate2)]

    row_spec = lambda w: pl.BlockSpec((tm, w), lambda i, j: (i, 0))
    mod_spec = pl.BlockSpec((None, 1, d), lambda i, j: (i // per, 0, 0))
    resident = lambda shape: pl.BlockSpec(shape, lambda i, j: (0,) * len(shape), pipeline_mode=pl.Buffered(1))
    seq = lambda i, j: (i * n_phase + j) // n_slab
    sl = lambda i, j: (i * n_phase + j) % n_slab
    new_spec = pl.BlockSpec((None, t, slab), lambda i, j: (seq(i, j), 0, sl(i, j)))
    win_spec = pl.BlockSpec((None, slab, wb), lambda i, j: (seq(i, j), sl(i, j), 0))
    tab_spec = pl.BlockSpec((t, nk), lambda i, j: (0, 0))
    return pl.pallas_call(
        functools.partial(_tail_sample_kernel, n_slab=n_slab, row_chunk=row_chunk, ff_chunk=ff_chunk),
        out_shape=[jax.ShapeDtypeStruct((rows, d), F32),
                   jax.ShapeDtypeStruct((db, t, a), F32),
                   jax.ShapeDtypeStruct((db, a, wb), F32),
                   jax.ShapeDtypeStruct((db, a, wb), F32)],
        grid=(n_tiles, n_phase),
        in_specs=[row_spec(d), row_spec(a), row_spec(p), mod_spec, mod_spec, mod_spec, mod_spec,
                  pl.BlockSpec((1, d), lambda i, j: (0, 0)),
                  resident((a + p, d)), resident((d, d_ff)), resident((d_ff, d)),
                  pl.BlockSpec(memory_space=pltpu.SMEM), tab_spec, tab_spec,
                  new_spec, new_spec, new_spec,
                  pl.BlockSpec(memory_space=pl.ANY), pl.BlockSpec(memory_space=pl.ANY)],
        out_specs=[row_spec(d), new_spec, win_spec, win_spec],
        scratch_shapes=[pltpu.VMEM((tm, d), BF16), pltpu.VMEM((tm, d), F32),
                        pltpu.VMEM((slab, nk), BF16), pltpu.VMEM((slab, nk), BF16),
                        pltpu.VMEM((n_slab, hps * t, nk), F32),
                        pltpu.VMEM((RING_DEPTH, slab, wb), F32), pltpu.VMEM((RING_DEPTH, slab, wb), F32),
                        pltpu.SemaphoreType.DMA((RING_DEPTH, 2))],
        compiler_params=pltpu.CompilerParams(dimension_semantics=("arbitrary", "arbitrary"),
                                             vmem_limit_bytes=VMEM_LIMIT_BIG),
        name="tail_sample",
    )(x2, attn2, pool2, *mods, g2, w_out_b, w_up_b, w_down_b,
      rel_bias, bkt, mult, q, k_new, v_new, cache_kt, cache_vt)


def _pool_kernel(u_ref, ctx_ref, wp_ref, ps_ref, o_ref, buf_a, buf_b, *, pos0):
    nb, t, p = u_ref.shape
    grp = p // len(POOL_WINDOWS)
    n = POOL_PAD + t
    ctx0 = POOL_LEAD + POOL_PAD - POOL_CTX
    buf_a[:, 0:ctx0, :] = jnp.zeros((nb, ctx0, p), F32)
    buf_b[:, 0:POOL_LEAD, :] = jnp.zeros((nb, POOL_LEAD, p), F32)
    buf_a[:, ctx0:ctx0 + POOL_CTX, :] = jnp.broadcast_to(ctx_ref[...], (nb, POOL_CTX, p))
    buf_a[:, ctx0 + POOL_CTX:, :] = u_ref[...]

    src, dst = buf_a, buf_b
    where = []
    for g, w in enumerate(POOL_WINDOWS):
        assert w == 2 ** (g + 1)
        cols = slice(g * grp, p)
        dst[:, POOL_LEAD:POOL_LEAD + n, cols] = (src[:, POOL_LEAD:POOL_LEAD + n, cols]
                                                  + src[:, POOL_LEAD - w // 2:POOL_LEAD - w // 2 + n, cols])
        where.append(dst)
        src, dst = dst, src

    pos = (pos0 + lax.broadcasted_iota(jnp.int32, (1, t, 1), 1)).astype(F32)
    new0 = POOL_LEAD + POOL_PAD
    for g, w in enumerate(POOL_WINDOWS):
        cols = slice(g * grp, (g + 1) * grp)
        win = where[g][:, new0:new0 + t, cols]
        cnt = jnp.minimum(float(w), pos + 1.0)
        pooled = (win / cnt - u_ref[:, :, cols]).reshape(nb * t, grp)
        y = jnp.dot(pooled.astype(BF16), wp_ref[g], preferred_element_type=F32)
        o_ref[:, :, cols] = (y * ps_ref[:, cols]).reshape(nb, t, grp).astype(o_ref.dtype)


def _pool_mix(u, ctx, w_pool_b, pool_scale, *, seqs_per_step, pos0):
    nb, t, p = u.shape
    g = len(POOL_WINDOWS)
    bb = seqs_per_step
    u_spec = pl.BlockSpec((bb, t, p), lambda i: (i, 0, 0))
    if ctx.shape[0] == 1:
        ctx_spec = pl.BlockSpec((1, POOL_CTX, p), lambda i: (0, 0, 0))
    else:
        ctx_spec = pl.BlockSpec((bb, POOL_CTX, p), lambda i: (i, 0, 0))
    return pl.pallas_call(
        functools.partial(_pool_kernel, pos0=pos0),
        out_shape=jax.ShapeDtypeStruct((nb, t, p), BF16),
        grid=(nb // bb,),
        in_specs=[u_spec, ctx_spec,
                  pl.BlockSpec((g, p // g, p // g), lambda i: (0, 0, 0)),
                  pl.BlockSpec((1, p), lambda i: (0, 0))],
        out_specs=u_spec,
        scratch_shapes=[pltpu.VMEM((bb, POOL_LEAD + POOL_PAD + t, p), F32)] * 2,
        compiler_params=pltpu.CompilerParams(dimension_semantics=("arbitrary",),
                                             vmem_limit_bytes=VMEM_LIMIT),
        name="pool_mix",
    )(u, ctx, w_pool_b, pool_scale)


def _tail_kernel(x_ref, attn_ref, pool_ref, gate1_ref, shift2_ref, scale2_ref, gate2_ref, g2_ref,
                 wout_ref, wup_ref, wdown_ref, y_ref, *, groups, ff_chunk):
    a = attn_ref.shape[1]
    d_ff = wup_ref.shape[1]
    mix = jnp.dot(attn_ref[...].astype(BF16), wout_ref[0:a, :], preferred_element_type=F32)
    mix = mix + jnp.dot(pool_ref[...].astype(BF16), wout_ref[a:, :], preferred_element_type=F32)
    x1 = x_ref[...] + _per_group(gate1_ref[...], mix, groups)
    hb = _modulated_norm(x1, g2_ref[...], shift2_ref[...], scale2_ref[...], groups).astype(BF16)
    f = jnp.zeros(x1.shape, F32)
    for c in range(d_ff // ff_chunk):
        cols = slice(c * ff_chunk, (c + 1) * ff_chunk)
        act = jnp.maximum(jnp.dot(hb, wup_ref[:, cols], preferred_element_type=F32), 0.0)
        f = f + jnp.dot((act * act).astype(BF16), wdown_ref[cols, :], preferred_element_type=F32)
    y_ref[...] = x1 + _per_group(gate2_ref[...], f, groups)


def _layer_tail(x2, attn2, pool2, gate1, shift2, scale2, gate2, g2, w_out_b, w_up_b, w_down_b,
                *, tm, rows_per_mod, ff_chunk=1024):
    rows, d = x2.shape
    a = attn2.shape[1]
    p = pool2.shape[1]
    d_ff = w_up_b.shape[1]
    mods = [gate1, shift2, scale2, gate2]
    if rows_per_mod >= tm:
        groups = 1
        per = rows_per_mod // tm
        mods = [m.reshape(-1, 1, d) for m in mods]
        mod_spec = pl.BlockSpec((None, 1, d), lambda i: (i // per, 0, 0))
    else:
        groups = tm // rows_per_mod
        mod_spec = pl.BlockSpec((groups, d), lambda i: (i, 0))
    const = lambda i: (0, 0)
    row_spec = lambda w: pl.BlockSpec((tm, w), lambda i: (i, 0))
    resident = lambda shape: pl.BlockSpec(shape, const, pipeline_mode=pl.Buffered(1))
    return pl.pallas_call(
        functools.partial(_tail_kernel, groups=groups, ff_chunk=ff_chunk),
        out_shape=jax.ShapeDtypeStruct((rows, d), F32),
        grid=(rows // tm,),
        in_specs=[row_spec(d), row_spec(a), row_spec(p), mod_spec, mod_spec, mod_spec, mod_spec,
                  pl.BlockSpec((1, d), const),
                  resident((a + p, d)), resident((d, d_ff)), resident((d_ff, d))],
        out_specs=row_spec(d),
        compiler_params=pltpu.CompilerParams(dimension_semantics=("arbitrary",),
                                             vmem_limit_bytes=VMEM_LIMIT),
        name="layer_tail",
    )(x2, attn2, pool2, *mods, g2, w_out_b, w_up_b, w_down_b)


def kernel(x_prompt, x_sample, c_prompt, c_sample, cache_k, cache_v, state_pool, w_ada, b_ada, norm1_g, norm2_g, w_in, q_norm_g, k_norm_g, rel_bias, w_pool, pool_scale, w_out, w_up, w_down):
    b, s_len, d = x_prompt.shape
    db, t, _ = x_sample.shape
    depth = w_ada.shape[0]
    assert depth == 1, "single-layer step"
    n_heads = rel_bias.shape[1]
    a = n_heads * HEAD_DIM
    wb = cache_k.shape[2]
    l = 0

    ada = _adaln(jnp.concatenate([c_prompt, c_sample], axis=0), w_ada[l], b_ada[l])
    mods = [ada[:, i * d:(i + 1) * d] for i in range(N_ADA)]
    mp = [m[:b] for m in mods]
    msm = [m[b:] for m in mods]

    w_in_b = w_in[l].astype(BF16)
    w_out_b = w_out[l].astype(BF16)
    w_up_b = w_up[l].astype(BF16)
    w_down_b = w_down[l].astype(BF16)
    w_pool_b = w_pool[l].astype(BF16)
    g1 = norm1_g[l].reshape(1, d)
    g2 = norm2_g[l].reshape(1, d)
    qg = jnp.tile(q_norm_g[l], n_heads).reshape(1, a)
    kg = jnp.tile(k_norm_g[l], n_heads).reshape(1, a)
    ps = pool_scale[l].reshape(1, -1)
    seg_i = jnp.arange(SEG_W) // HEAD_DIM
    seg = jnp.where(seg_i[:, None] == seg_i[None, :], 1.0 / HEAD_DIM, 0.0).astype(BF16)

    xp2 = x_prompt.reshape(b * s_len, d)
    q, k, v, u, kt, vt = _mixer_in(xp2, mp[0], mp[1], g1, w_in_b, qg, kg, seg, tm=512, rows_per_mod=s_len,
                                   feature_major=True)
    attn = _prompt_attention(q.reshape(b, s_len, a), k.reshape(b, s_len, a), v.reshape(b, s_len, a), rel_bias)
    u3 = u.reshape(b, s_len, -1)
    pool = _pool_mix(u3, jnp.zeros((1, POOL_CTX, u3.shape[2]), F32), w_pool_b, ps, seqs_per_step=1, pos0=0)
    keep = min(max(w for w, _ in DILATED_BRANCHES), s_len)
    win = lambda c: jnp.transpose(c.reshape(b, n_heads, HEAD_DIM, s_len), (0, 3, 1, 2))[None, :, s_len - keep:]
    k_win_prompt = win(kt)
    v_win_prompt = win(vt)
    pool_prompt = u3[None, :, s_len - POOL_CTX:]

    xs2 = x_sample.reshape(db * t, d)
    qs, ks, vs, us = _mixer_in(xs2, msm[0], msm[1], g1, w_in_b, qg, kg, seg, tm=256, rows_per_mod=t)
    to_fm = lambda c: jnp.transpose(c, (0, 2, 3, 1)).reshape(db, a, wb)
    from_fm = lambda c: jnp.transpose(c.reshape(db, n_heads, HEAD_DIM, wb), (0, 3, 1, 2))[None]
    y_prompt, attn_s, k_win, v_win = _tail_and_sample_attention(
        xp2, attn.reshape(b * s_len, a), pool.reshape(b * s_len, -1),
        mp[2], mp[3], mp[4], mp[5], g2, w_out_b, w_up_b, w_down_b,
        qs.reshape(db, t, a), ks.reshape(db, t, a), vs.reshape(db, t, a),
        to_fm(cache_k[l]), to_fm(cache_v[l]), rel_bias, tm=512, rows_per_mod=s_len)
    y_prompt = y_prompt.reshape(b, s_len, d)
    us3 = us.reshape(db, t, -1)
    pool_s = _pool_mix(us3, state_pool[l], w_pool_b, ps, seqs_per_step=db, pos0=PAST_LEN)
    y_sample = _layer_tail(xs2, attn_s.reshape(db * t, a), pool_s.reshape(db * t, -1),
                           msm[2], msm[3], msm[4], msm[5], g2, w_out_b, w_up_b, w_down_b,
                           tm=256, rows_per_mod=t).reshape(db, t, d)
    k_win_sample = from_fm(k_win)
    v_win_sample = from_fm(v_win)
    pool_sample = jnp.concatenate([state_pool[l], us3], axis=1)[None, :, -POOL_CTX:]

    return (y_prompt, y_sample, k_win_prompt, v_win_prompt, pool_prompt,
            k_win_sample, v_win_sample, pool_sample)
```

```python
import functools
import math

import jax
import jax.numpy as jnp
from jax import lax
from jax.experimental import pallas as pl
from jax.experimental.pallas import tpu as pltpu

F32 = jnp.float32
BF16 = jnp.bfloat16

HEAD_DIM = 64
DILATED_BRANCHES = ((128, 1), (512, 4), (2048, 16))
NUM_BUCKETS = 32
MAX_DISTANCE = 2048
POOL_WINDOWS = (2, 4, 8, 16)
PAST_LEN = 8192
POOL_CTX = max(POOL_WINDOWS) - 1
POOL_PAD = 16
POOL_LEAD = max(POOL_WINDOWS) // 2
N_ADA = 6
EPS = 1e-6
NEG_INF = -1e30
LOG2E = math.log2(math.e)

LANES = 128
SEG_W = 256
VMEM_LIMIT = 56 * 1024 * 1024
VMEM_LIMIT_BIG = 62 * 1024 * 1024
RING_DEPTH = 3


def _rel_bucket(dist):
    exact = NUM_BUCKETS // 2
    d = jnp.maximum(dist.astype(F32), 1.0)
    large = exact + (jnp.log(d / exact) / math.log(MAX_DISTANCE / exact)
                     * (NUM_BUCKETS - exact)).astype(jnp.int32)
    large = jnp.minimum(large, NUM_BUCKETS - 1)
    return jnp.where(dist < exact, dist, large)


def _bias_from_buckets(bkt, relb_ref, head):
    out = jnp.full(bkt.shape, NEG_INF, F32)
    for b in range(NUM_BUCKETS):
        out = jnp.where(bkt == b, relb_ref[b, head], out)
    return out


def _adaln_kernel(c_ref, w_ref, b_ref, o_ref):
    c = c_ref[...]
    s = c / (1.0 + jnp.exp(-c))
    o_ref[...] = jnp.dot(s, w_ref[...], preferred_element_type=F32,
                         precision=lax.Precision.HIGHEST) + b_ref[...]


def _adaln(c, w, b, tn=1024):
    m, d = c.shape
    n = w.shape[1]
    return pl.pallas_call(
        _adaln_kernel,
        out_shape=jax.ShapeDtypeStruct((m, n), F32),
        grid=(n // tn,),
        in_specs=[pl.BlockSpec((m, d), lambda j: (0, 0)),
                  pl.BlockSpec((d, tn), lambda j: (0, j)),
                  pl.BlockSpec((1, tn), lambda j: (0, j))],
        out_specs=pl.BlockSpec((m, tn), lambda j: (0, j)),
        compiler_params=pltpu.CompilerParams(dimension_semantics=("arbitrary",),
                                             vmem_limit_bytes=VMEM_LIMIT),
        name="adaln",
    )(c, w, b.reshape(1, n))


def _modulated_norm(x, g, shift, scale, groups):
    tm, d = x.shape
    ms = jnp.mean(x * x, axis=-1, keepdims=True)
    y = x * lax.rsqrt(ms + EPS) * g
    if groups == 1:
        return y * (1.0 + scale) + shift
    y3 = y.reshape(groups, tm // groups, d)
    return (y3 * (1.0 + scale[:, None, :]) + shift[:, None, :]).reshape(tm, d)


def _per_group(v, x, groups):
    tm, d = x.shape
    if groups == 1:
        return v * x
    return (v[:, None, :] * x.reshape(groups, tm // groups, d)).reshape(tm, d)


def _pool_mix(u3, ctx, wp_ref, ps_ref, buf_a, buf_b, pos):
    nb, t, p = u3.shape
    grp = p // len(POOL_WINDOWS)
    n = POOL_PAD + t
    ctx0 = POOL_LEAD + POOL_PAD - POOL_CTX
    buf_a[:, 0:ctx0, :] = jnp.zeros((nb, ctx0, p), F32)
    buf_b[:, 0:POOL_LEAD, :] = jnp.zeros((nb, POOL_LEAD, p), F32)
    buf_a[:, ctx0:ctx0 + POOL_CTX, :] = ctx
    buf_a[:, ctx0 + POOL_CTX:, :] = u3
    last = buf_a[:, POOL_LEAD + n - POOL_PAD:POOL_LEAD + n, :]

    src, dst = buf_a, buf_b
    where = []
    for g, w in enumerate(POOL_WINDOWS):
        assert w == 2 ** (g + 1)
        cols = slice(g * grp, p)
        dst[:, POOL_LEAD:POOL_LEAD + n, cols] = (src[:, POOL_LEAD:POOL_LEAD + n, cols]
                                                  + src[:, POOL_LEAD - w // 2:POOL_LEAD - w // 2 + n, cols])
        where.append(dst)
        src, dst = dst, src

    new0 = POOL_LEAD + POOL_PAD
    outs = []
    for g, w in enumerate(POOL_WINDOWS):
        cols = slice(g * grp, (g + 1) * grp)
        win = where[g][:, new0:new0 + t, cols]
        cnt = jnp.minimum(float(w), pos + 1.0)
        pooled = (win / cnt - u3[:, :, cols]).reshape(nb * t, grp)
        y = jnp.dot(pooled.astype(BF16), wp_ref[g], preferred_element_type=F32)
        outs.append((y * ps_ref[:, cols]).reshape(nb, t, grp))
    return jnp.concatenate(outs, axis=-1), last


def _mixer_in_kernel(x_ref, shift_ref, scale_ref, g1_ref, w_ref, qg_ref, kg_ref, seg_ref, wp_ref, ps_ref,
                     *refs, groups, tiles_per_seq, pos0, feature_major):
    refs = list(refs)
    ctx_ref = None if tiles_per_seq else refs.pop(0)
    q_ref, k_ref, v_ref, pool_ref, last_ref = refs[:5]
    refs = refs[5:]
    if feature_major:
        kt_ref, vt_ref = refs[:2]
        refs = refs[2:]
    buf_a, buf_b = refs[:2]
    tm, a = q_ref.shape
    p = pool_ref.shape[1]
    t = tm // groups
    row = lax.broadcasted_iota(jnp.int32, (1, t, 1), 1)
    if tiles_per_seq:
        carry = refs[2]
        tile = pl.program_id(0) % tiles_per_seq

        @pl.when(tile == 0)
        def _():
            carry[...] = jnp.zeros(carry.shape, F32)

        ctx = carry[:, POOL_PAD - POOL_CTX:, :]
        pos = (pos0 + tile * tm + row).astype(F32)
    else:
        ctx = ctx_ref[...]
        pos = (pos0 + row).astype(F32)

    h = _modulated_norm(x_ref[...], g1_ref[...], shift_ref[...], scale_ref[...], groups)
    hb = h.astype(BF16)

    def head_norm(z, g):
        parts = []
        for c in range(a // SEG_W):
            zc = z[:, c * SEG_W:(c + 1) * SEG_W]
            ms = jnp.dot((zc * zc).astype(BF16), seg_ref[...], preferred_element_type=F32)
            parts.append(zc * lax.rsqrt(ms + EPS))
        return jnp.concatenate(parts, axis=-1) * g

    zq = jnp.dot(hb, w_ref[:, 0:a], preferred_element_type=F32)
    q_ref[...] = head_norm(zq, qg_ref[...]) * (HEAD_DIM ** -0.5)
    zk = jnp.dot(hb, w_ref[:, a:2 * a], preferred_element_type=F32)
    k = head_norm(zk, kg_ref[...])
    k_ref[...] = k
    v = jnp.dot(hb, w_ref[:, 2 * a:3 * a], preferred_element_type=F32)
    v_ref[...] = v
    if feature_major:
        kt_ref[...] = k.T
        vt_ref[...] = v.T

    u = jnp.dot(hb, w_ref[:, 3 * a:], preferred_element_type=F32)
    pool, last = _pool_mix(u.reshape(groups, t, p), ctx, wp_ref, ps_ref, buf_a, buf_b, pos)
    pool_ref[...] = pool.reshape(tm, p).astype(pool_ref.dtype)
    last_ref[...] = last
    if tiles_per_seq:
        carry[...] = last


def _mixer_in(x2, shift, scale, g1, w_in_b, qg, kg, seg, w_pool_b, pool_scale, ctx, *, tm, rows_per_mod, pos0,
              feature_major=False):
    rows, d = x2.shape
    d_in = w_in_b.shape[1]
    a = qg.shape[1]
    p = d_in - 3 * a
    n_seq = rows // rows_per_mod
    g = len(POOL_WINDOWS)
    const = lambda i: (0, 0)
    row_spec = lambda w: pl.BlockSpec((tm, w), lambda i: (i, 0))
    in_specs = [row_spec(d), None, None,
                pl.BlockSpec((1, d), const),
                pl.BlockSpec((d, d_in), const),
                pl.BlockSpec((1, a), const), pl.BlockSpec((1, a), const),
                pl.BlockSpec((SEG_W, SEG_W), const),
                pl.BlockSpec((g, p // g, p // g), lambda i: (0, 0, 0)),
                pl.BlockSpec((1, p), const)]
    args = [x2, shift, scale, g1, w_in_b, qg, kg, seg, w_pool_b, pool_scale]
    if rows_per_mod >= tm:
        assert ctx is None and rows_per_mod % tm == 0
        groups = 1
        per = rows_per_mod // tm
        args[1] = shift.reshape(-1, 1, d)
        args[2] = scale.reshape(-1, 1, d)
        in_specs[1] = in_specs[2] = pl.BlockSpec((None, 1, d), lambda i: (i // per, 0, 0))
        last_spec = pl.BlockSpec((1, POOL_PAD, p), lambda i: (i // per, 0, 0))
        scratch = [pltpu.VMEM((1, POOL_PAD, p), F32)]
    else:
        assert ctx is not None and not feature_major and tm % rows_per_mod == 0
        groups = tm // rows_per_mod
        per = None
        in_specs[1] = in_specs[2] = pl.BlockSpec((groups, d), lambda i: (i, 0))
        in_specs.append(pl.BlockSpec((groups, POOL_CTX, p), lambda i: (i, 0, 0)))
        args.append(ctx)
        last_spec = pl.BlockSpec((groups, POOL_PAD, p), lambda i: (i, 0, 0))
        scratch = []
    out_shape = ([jax.ShapeDtypeStruct((rows, a), F32)] * 3
                 + [jax.ShapeDtypeStruct((rows, p), BF16), jax.ShapeDtypeStruct((n_seq, POOL_PAD, p), F32)])
    out_specs = [row_spec(a)] * 3 + [row_spec(p), last_spec]
    if feature_major:
        out_shape += [jax.ShapeDtypeStruct((n_seq, a, rows_per_mod), F32)] * 2
        out_specs += [pl.BlockSpec((None, a, tm), lambda i: (i // per, 0, i % per))] * 2
    pool_buf = pltpu.VMEM((groups, POOL_LEAD + POOL_PAD + tm // groups, p), F32)
    return pl.pallas_call(
        functools.partial(_mixer_in_kernel, groups=groups, tiles_per_seq=per, pos0=pos0,
                          feature_major=feature_major),
        out_shape=out_shape,
        grid=(rows // tm,),
        in_specs=in_specs,
        out_specs=out_specs,
        scratch_shapes=[pool_buf, pool_buf] + scratch,
        compiler_params=pltpu.CompilerParams(dimension_semantics=("arbitrary",),
                                             vmem_limit_bytes=VMEM_LIMIT),
        name="mixer_in",
    )(*args)


def _prompt_attn_kernel(relb_ref, bkt_ref, q_ref, k_ref, v_ref, o_ref,
                        qd, kd, vd, bias_s, acc_s, m_s, l_s, *, blk, group):
    s_len = q_ref.shape[0]
    nbr = len(DILATED_BRANCHES)
    mid = DILATED_BRANCHES[1][1]
    msub = s_len // mid
    pair = pl.program_id(0)
    head0 = lax.broadcasted_iota(jnp.int32, (1, LANES), 1) < HEAD_DIM

    @pl.when(pl.program_id(1) == 0)
    def _():
        for bi in range(nbr):
            for hh in range(2):
                bias_s[bi, hh * blk:(hh + 1) * blk, :] = LOG2E * _bias_from_buckets(bkt_ref[bi], relb_ref,
                                                                                      2 * pair + hh)

    for bi, (_, dil) in enumerate(DILATED_BRANCHES):
        sub = s_len // dil
        for src, dst, scale in ((q_ref, qd, LOG2E), (k_ref, kd, None), (v_ref, vd, None)):
            for r in range(dil):
                rows = src[...] if dil == 1 else src[pl.ds(r, sub, stride=dil), :]
                if scale is not None:
                    rows = rows * scale
                dst[bi, r * sub:(r + 1) * sub, :] = rows.astype(BF16)

    def unit(bi, base, dst, first):
        nk = blk if first else 2 * blk
        kbase = base if first else base - blk
        qb = qd[bi, pl.ds(base, blk), :]
        zero = jnp.zeros_like(qb)
        q2 = jnp.concatenate([jnp.where(head0, qb, zero), jnp.where(head0, zero, qb)], axis=0)
        s = lax.dot_general(q2, kd[bi, pl.ds(kbase, nk), :], (((1,), (1,)), ((), ())),
                            preferred_element_type=F32)
        s = s + (bias_s[bi, :, blk:] if first else bias_s[bi])
        m = jnp.max(s, axis=-1, keepdims=True)
        p = jnp.exp2(s - m)
        l = jnp.sum(p, axis=-1, keepdims=True)
        acc = jnp.dot(p.astype(BF16), vd[bi, pl.ds(kbase, nk), :], preferred_element_type=F32)
        full = (blk, LANES)
        acc_s[bi, dst, :] = jnp.where(head0, acc[:blk], acc[blk:])
        m_s[bi, dst, :] = jnp.where(head0, jnp.broadcast_to(m[:blk], full), jnp.broadcast_to(m[blk:], full))
        l_s[bi, dst, :] = jnp.where(head0, jnp.broadcast_to(l[:blk], full), jnp.broadcast_to(l[blk:], full))

    for bi, (_, dil) in enumerate(DILATED_BRANCHES):
        sub = s_len // dil
        nb = sub // blk
        if dil <= mid:
            def residue(r, carry, bi=bi, sub=sub, nb=nb):
                base = pl.multiple_of(r * sub, blk)
                unit(bi, base, pl.ds(base, blk), True)

                def later(n, c):
                    b2 = pl.multiple_of(r * sub + n * blk, blk)
                    unit(bi, b2, pl.ds(b2, blk), False)
                    return c

                lax.fori_loop(1, nb, later, 0, unroll=min(group, nb - 1))
                return carry

            lax.fori_loop(0, dil, residue, 0, unroll=max(1, min(dil, group // nb)))
        else:
            step = dil // mid
            for r in range(dil):
                for n in range(nb):
                    start = (r % mid) * msub + r // mid + n * blk * step
                    unit(bi, r * sub + n * blk, pl.ds(start, blk, stride=step), n == 0)

    for r in range(mid):
        for c in range(msub // blk):
            res = slice(r * msub + c * blk, r * msub + (c + 1) * blk)
            nat = pl.ds(c * blk * mid + r, blk, stride=mid)
            rows = [nat] + [res] * (nbr - 1)
            m = [m_s[bi, rows[bi], :] for bi in range(nbr)]
            top = functools.reduce(jnp.maximum, m)
            num = jnp.zeros((blk, LANES), F32)
            den = jnp.zeros((blk, LANES), F32)
            for bi in range(nbr):
                w = jnp.exp2(m[bi] - top)
                num = num + w * acc_s[bi, rows[bi], :]
                den = den + w * l_s[bi, rows[bi], :]
            o_ref[nat, :] = num / den


def _prompt_attention(q, k, v, rel_bias, blk=128, group=16):
    b, s_len, a = q.shape
    n_pairs = a // LANES
    nbr = len(DILATED_BRANCHES)
    assert all(w // d == blk and (s_len // d) % blk == 0 for w, d in DILATED_BRANCHES)
    qi = jnp.arange(blk)[:, None]
    ki = jnp.arange(2 * blk)[None, :]
    dist = qi + blk - ki
    ok = (dist >= 0) & (dist <= blk)
    bkt = jnp.stack([jnp.where(ok, _rel_bucket(jnp.clip(dist, 0, blk) * d), -1)
                     for _, d in DILATED_BRANCHES]).astype(jnp.int32)
    seq_spec = pl.BlockSpec((None, s_len, LANES), lambda p, i: (i, 0, p))
    return pl.pallas_call(
        functools.partial(_prompt_attn_kernel, blk=blk, group=group),
        out_shape=jax.ShapeDtypeStruct((b, s_len, a), F32),
        grid=(n_pairs, b),
        in_specs=[pl.BlockSpec(memory_space=pltpu.SMEM),
                  pl.BlockSpec((nbr, blk, 2 * blk), lambda p, i: (0, 0, 0)),
                  seq_spec, seq_spec, seq_spec],
        out_specs=seq_spec,
        scratch_shapes=[pltpu.VMEM((nbr, s_len, LANES), BF16)] * 3
                       + [pltpu.VMEM((nbr, 2 * blk, 2 * blk), F32)]
                       + [pltpu.VMEM((nbr, s_len, LANES), F32)] * 3,
        compiler_params=pltpu.CompilerParams(dimension_semantics=("arbitrary", "arbitrary"),
                                             vmem_limit_bytes=VMEM_LIMIT),
        name="prompt_attn",
    )(rel_bias, bkt, q, k, v)


def _sample_tables(t, wb, nk):
    dist = wb + jnp.arange(t)[:, None] - jnp.arange(nk)[None, :]
    mult = jnp.zeros((t, nk), jnp.int32)
    for w, d in DILATED_BRANCHES:
        mult = mult + ((dist >= 0) & (dist <= w) & (dist % d == 0)).astype(jnp.int32)
    bkt = jnp.where(mult > 0, _rel_bucket(jnp.maximum(dist, 0)), -1).astype(jnp.int32)
    return bkt, mult.astype(F32)


def _sample_slab(bias, mult, q_ref, kn_ref, vn_ref, ck_ref, cv_ref, o_ref, ko_ref, vo_ref, kall, vall,
                 row_chunk):
    t, slab = q_ref.shape
    wb = ck_ref.shape[1]
    nk = kall.shape[1]
    pad = nk - wb
    hps = slab // HEAD_DIM

    def append(new_ref, cache_ref, out_ref, all_ref):
        new_t = jnp.concatenate([new_ref[...], jnp.zeros((pad - t, slab), F32)], axis=0).T
        for c in range(slab // row_chunk):
            rows = slice(c * row_chunk, (c + 1) * row_chunk)
            ext = jnp.concatenate([cache_ref[rows, :], new_t[rows, :]], axis=1)
            all_ref[rows, :] = ext.astype(BF16)
            out_ref[rows, :] = pltpu.roll(ext, nk - t, axis=1)[:, 0:wb]

    append(kn_ref, ck_ref, ko_ref, kall)
    append(vn_ref, cv_ref, vo_ref, vall)

    lane = lax.broadcasted_iota(jnp.int32, (hps * t, slab), 1)
    row = lax.broadcasted_iota(jnp.int32, (hps * t, slab), 0)
    own = (lane // HEAD_DIM) == (row // t)
    qs = jnp.where(own, jnp.concatenate([q_ref[...]] * hps, axis=0), 0.0).astype(BF16)
    s = jnp.dot(qs, kall[...], preferred_element_type=F32) + bias
    m = jnp.max(s, axis=-1, keepdims=True)
    p = jnp.exp(s - m) * jnp.concatenate([mult] * hps, axis=0)
    l = jnp.sum(p, axis=-1, keepdims=True)
    o = lax.dot_general(p.astype(BF16), vall[...], (((1,), (1,)), ((), ())),
                        preferred_element_type=F32) / l
    o = jnp.where(own, o, 0.0)
    res = o[0:t]
    for hh in range(1, hps):
        res = res + o[hh * t:(hh + 1) * t]
    o_ref[...] = res


def _tail_sample_kernel(x_ref, attn_ref, pool_ref, gate1_ref, shift2_ref, scale2_ref, gate2_ref, g2_ref,
                        wout_ref, wup_ref, wdown_ref,
                        relb_ref, bkt_ref, mult_ref, q_ref, kn_ref, vn_ref, ck_ref, cv_ref,
                        y_ref, o_ref, ko_ref, vo_ref,
                        hb_s, acc_s, kall, vall, bias_s, ring_k, ring_v, ring_sem,
                        *, n_slab, row_chunk, ff_chunk):
    i = pl.program_id(0)
    j = pl.program_id(1)
    n_phase = pl.num_programs(1)
    a = attn_ref.shape[1]
    t, slab = q_ref.shape
    hps = slab // HEAD_DIM
    step = i * n_phase + j
    n_steps = pl.num_programs(0) * n_phase

    def window_copies(unit, slot):
        seq = unit // n_slab
        rows = pl.ds(pl.multiple_of((unit % n_slab) * slab, slab), slab)
        return [pltpu.make_async_copy(src.at[seq, rows, :], ring.at[slot], ring_sem.at[slot, which])
                for which, (src, ring) in enumerate(((ck_ref, ring_k), (cv_ref, ring_v)))]

    @pl.when(step == 0)
    def _():
        for u in range(RING_DEPTH - 1):
            for c in window_copies(u, u):
                c.start()

    ahead = step + (RING_DEPTH - 1)

    @pl.when(ahead < n_steps)
    def _():
        for c in window_copies(ahead, ahead % RING_DEPTH):
            c.start()

    @pl.when((i == 0) & (j == 0))
    def _():
        for h in range(n_slab * hps):
            bias_s[h // hps, (h % hps) * t:(h % hps + 1) * t, :] = _bias_from_buckets(bkt_ref[...], relb_ref, h)

    @pl.when(j == 0)
    def _():
        mix = jnp.dot(attn_ref[...].astype(BF16), wout_ref[0:a, :], preferred_element_type=F32)
        mix = mix + jnp.dot(pool_ref[...].astype(BF16), wout_ref[a:, :], preferred_element_type=F32)
        x1 = x_ref[...] + gate1_ref[...] * mix
        y_ref[...] = x1
        hb_s[...] = _modulated_norm(x1, g2_ref[...], shift2_ref[...], scale2_ref[...], 1).astype(BF16)
        acc_s[...] = jnp.zeros(acc_s.shape, F32)

    slot = step % RING_DEPTH
    for c in window_copies(step, slot):
        c.wait()
    _sample_slab(bias_s[step % n_slab], mult_ref[...], q_ref, kn_ref, vn_ref, ring_k.at[slot], ring_v.at[slot],
                 o_ref, ko_ref, vo_ref, kall, vall, row_chunk)
    ff = pl.ds(pl.multiple_of(j * ff_chunk, ff_chunk), ff_chunk)
    act = jnp.maximum(jnp.dot(hb_s[...], wup_ref[:, ff], preferred_element_type=F32), 0.0)
    acc_s[...] += jnp.dot((act * act).astype(BF16), wdown_ref[ff, :], preferred_element_type=F32)

    @pl.when(j == n_phase - 1)
    def _():
        y_ref[...] = y_ref[...] + gate2_ref[...] * acc_s[...]


def _tail_and_sample_attention(x2, attn2, pool2, gate1, shift2, scale2, gate2, g2, w_out_b, w_up_b, w_down_b,
                               q, k_new, v_new, cache_kt, cache_vt, rel_bias,
                               *, tm, rows_per_mod, ff_chunk=1024, slab=256, pad=128, row_chunk=64):
    rows, d = x2.shape
    a = attn2.shape[1]
    p = pool2.shape[1]
    d_ff = w_up_b.shape[1]
    db, t, _ = q.shape
    wb = cache_kt.shape[2]
    nk = wb + pad
    n_slab = a // slab
    n_phase = d_ff // ff_chunk
    n_tiles = rows // tm
    hps = slab // HEAD_DIM
    assert pad >= t and wb >= max(w for w, _ in DILATED_BRANCHES)
    assert n_tiles * n_phase == db * n_slab and rows_per_mod % tm == 0
    per = rows_per_mod // tm
    bkt, mult = _sample_tables(t, wb, nk)
    mods = [m.reshape(-1, 1, d) for m in (gate1, shift2, scale2, gate2)]

    row_spec = lambda w: pl.BlockSpec((tm, w), lambda i, j: (i, 0))
    mod_spec = pl.BlockSpec((None, 1, d), lambda i, j: (i // per, 0, 0))
    resident = lambda shape: pl.BlockSpec(shape, lambda i, j: (0,) * len(shape), pipeline_mode=pl.Buffered(1))
    seq = lambda i, j: (i * n_phase + j) // n_slab
    sl = lambda i, j: (i * n_phase + j) % n_slab
    new_spec = pl.BlockSpec((None, t, slab), lambda i, j: (seq(i, j), 0, sl(i, j)))
    win_spec = pl.BlockSpec((None, slab, wb), lambda i, j: (seq(i, j), sl(i, j), 0))
    tab_spec = pl.BlockSpec((t, nk), lambda i, j: (0, 0))
    return pl.pallas_call(
        functools.partial(_tail_sample_kernel, n_slab=n_slab, row_chunk=row_chunk, ff_chunk=ff_chunk),
        out_shape=[jax.ShapeDtypeStruct((rows, d), F32),
                   jax.ShapeDtypeStruct((db, t, a), F32),
                   jax.ShapeDtypeStruct((db, a, wb), F32),
                   jax.ShapeDtypeStruct((db, a, wb), F32)],
        grid=(n_tiles, n_phase),
        in_specs=[row_spec(d), row_spec(a), row_spec(p), mod_spec, mod_spec, mod_spec, mod_spec,
                  pl.BlockSpec((1, d), lambda i, j: (0, 0)),
                  resident((a + p, d)), resident((d, d_ff)), resident((d_ff, d)),
                  pl.BlockSpec(memory_space=pltpu.SMEM), tab_spec, tab_spec,
                  new_spec, new_spec, new_spec,
                  pl.BlockSpec(memory_space=pl.ANY), pl.BlockSpec(memory_space=pl.ANY)],
        out_specs=[row_spec(d), new_spec, win_spec, win_spec],
        scratch_shapes=[pltpu.VMEM((tm, d), BF16), pltpu.VMEM((tm, d), F32),
                        pltpu.VMEM((slab, nk), BF16), pltpu.VMEM((slab, nk), BF16),
                        pltpu.VMEM((n_slab, hps * t, nk), F32),
                        pltpu.VMEM((RING_DEPTH, slab, wb), F32), pltpu.VMEM((RING_DEPTH, slab, wb), F32),
                        pltpu.SemaphoreType.DMA((RING_DEPTH, 2))],
        compiler_params=pltpu.CompilerParams(dimension_semantics=("arbitrary", "arbitrary"),
                                             vmem_limit_bytes=VMEM_LIMIT_BIG),
        name="tail_sample",
    )(x2, attn2, pool2, *mods, g2, w_out_b, w_up_b, w_down_b,
      rel_bias, bkt, mult, q, k_new, v_new, cache_kt, cache_vt)


def _tail_kernel(x_ref, attn_ref, pool_ref, gate1_ref, shift2_ref, scale2_ref, gate2_ref, g2_ref,
                 wout_ref, wup_ref, wdown_ref, y_ref, *, groups, ff_chunk):
    a = attn_ref.shape[1]
    d_ff = wup_ref.shape[1]
    mix = jnp.dot(attn_ref[...].astype(BF16), wout_ref[0:a, :], preferred_element_type=F32)
    mix = mix + jnp.dot(pool_ref[...].astype(BF16), wout_ref[a:, :], preferred_element_type=F32)
    x1 = x_ref[...] + _per_group(gate1_ref[...], mix, groups)
    hb = _modulated_norm(x1, g2_ref[...], shift2_ref[...], scale2_ref[...], groups).astype(BF16)
    f = jnp.zeros(x1.shape, F32)
    for c in range(d_ff // ff_chunk):
        cols = slice(c * ff_chunk, (c + 1) * ff_chunk)
        act = jnp.maximum(jnp.dot(hb, wup_ref[:, cols], preferred_element_type=F32), 0.0)
        f = f + jnp.dot((act * act).astype(BF16), wdown_ref[cols, :], preferred_element_type=F32)
    y_ref[...] = x1 + _per_group(gate2_ref[...], f, groups)


def _layer_tail(x2, attn2, pool2, gate1, shift2, scale2, gate2, g2, w_out_b, w_up_b, w_down_b,
                *, tm, rows_per_mod, ff_chunk=1024):
    rows, d = x2.shape
    a = attn2.shape[1]
    p = pool2.shape[1]
    d_ff = w_up_b.shape[1]
    mods = [gate1, shift2, scale2, gate2]
    if rows_per_mod >= tm:
        groups = 1
        per = rows_per_mod // tm
        mods = [m.reshape(-1, 1, d) for m in mods]
        mod_spec = pl.BlockSpec((None, 1, d), lambda i: (i // per, 0, 0))
    else:
        groups = tm // rows_per_mod
        mod_spec = pl.BlockSpec((groups, d), lambda i: (i, 0))
    const = lambda i: (0, 0)
    row_spec = lambda w: pl.BlockSpec((tm, w), lambda i: (i, 0))
    resident = lambda shape: pl.BlockSpec(shape, const, pipeline_mode=pl.Buffered(1))
    return pl.pallas_call(
        functools.partial(_tail_kernel, groups=groups, ff_chunk=ff_chunk),
        out_shape=jax.ShapeDtypeStruct((rows, d), F32),
        grid=(rows // tm,),
        in_specs=[row_spec(d), row_spec(a), row_spec(p), mod_spec, mod_spec, mod_spec, mod_spec,
                  pl.BlockSpec((1, d), const),
                  resident((a + p, d)), resident((d, d_ff)), resident((d_ff, d))],
        out_specs=row_spec(d),
        compiler_params=pltpu.CompilerParams(dimension_semantics=("arbitrary",),
                                             vmem_limit_bytes=VMEM_LIMIT),
        name="layer_tail",
    )(x2, attn2, pool2, *mods, g2, w_out_b, w_up_b, w_down_b)


def kernel(x_prompt, x_sample, c_prompt, c_sample, cache_k, cache_v, state_pool, w_ada, b_ada, norm1_g, norm2_g, w_in, q_norm_g, k_norm_g, rel_bias, w_pool, pool_scale, w_out, w_up, w_down):
    b, s_len, d = x_prompt.shape
    db, t, _ = x_sample.shape
    depth = w_ada.shape[0]
    assert depth == 1, "single-layer step"
    n_heads = rel_bias.shape[1]
    a = n_heads * HEAD_DIM
    wb = cache_k.shape[2]
    l = 0

    ada = _adaln(jnp.concatenate([c_prompt, c_sample], axis=0), w_ada[l], b_ada[l])
    mods = [ada[:, i * d:(i + 1) * d] for i in range(N_ADA)]
    mp = [m[:b] for m in mods]
    msm = [m[b:] for m in mods]

    w_in_b = w_in[l].astype(BF16)
    w_out_b = w_out[l].astype(BF16)
    w_up_b = w_up[l].astype(BF16)
    w_down_b = w_down[l].astype(BF16)
    w_pool_b = w_pool[l].astype(BF16)
    g1 = norm1_g[l].reshape(1, d)
    g2 = norm2_g[l].reshape(1, d)
    qg = jnp.tile(q_norm_g[l], n_heads).reshape(1, a)
    kg = jnp.tile(k_norm_g[l], n_heads).reshape(1, a)
    ps = pool_scale[l].reshape(1, -1)
    seg_i = jnp.arange(SEG_W) // HEAD_DIM
    seg = jnp.where(seg_i[:, None] == seg_i[None, :], 1.0 / HEAD_DIM, 0.0).astype(BF16)

    xp2 = x_prompt.reshape(b * s_len, d)
    q, k, v, pool, u_last, kt, vt = _mixer_in(xp2, mp[0], mp[1], g1, w_in_b, qg, kg, seg, w_pool_b, ps, None,
                                              tm=512, rows_per_mod=s_len, pos0=0, feature_major=True)
    attn = _prompt_attention(q.reshape(b, s_len, a), k.reshape(b, s_len, a), v.reshape(b, s_len, a), rel_bias)
    keep = min(max(w for w, _ in DILATED_BRANCHES), s_len)
    win = lambda c: jnp.transpose(c.reshape(b, n_heads, HEAD_DIM, s_len), (0, 3, 1, 2))[None, :, s_len - keep:]
    k_win_prompt = win(kt)
    v_win_prompt = win(vt)
    pool_prompt = u_last[None, :, POOL_PAD - POOL_CTX:]

    xs2 = x_sample.reshape(db * t, d)
    qs, ks, vs, pool_s, us_last = _mixer_in(xs2, msm[0], msm[1], g1, w_in_b, qg, kg, seg, w_pool_b, ps,
                                            state_pool[l], tm=256, rows_per_mod=t, pos0=PAST_LEN)
    to_fm = lambda c: jnp.transpose(c, (0, 2, 3, 1)).reshape(db, a, wb)
    from_fm = lambda c: jnp.transpose(c.reshape(db, n_heads, HEAD_DIM, wb), (0, 3, 1, 2))[None]
    y_prompt, attn_s, k_win, v_win = _tail_and_sample_attention(
        xp2, attn.reshape(b * s_len, a), pool,
        mp[2], mp[3], mp[4], mp[5], g2, w_out_b, w_up_b, w_down_b,
        qs.reshape(db, t, a), ks.reshape(db, t, a), vs.reshape(db, t, a),
        to_fm(cache_k[l]), to_fm(cache_v[l]), rel_bias, tm=512, rows_per_mod=s_len)
    y_prompt = y_prompt.reshape(b, s_len, d)
    y_sample = _layer_tail(xs2, attn_s.reshape(db * t, a), pool_s,
                           msm[2], msm[3], msm[4], msm[5], g2, w_out_b, w_up_b, w_down_b,
                           tm=256, rows_per_mod=t).reshape(db, t, d)
    k_win_sample = from_fm(k_win)
    v_win_sample = from_fm(v_win)
    pool_sample = us_last[None, :, POOL_PAD - POOL_CTX:]

    return (y_prompt, y_sample, k_win_prompt, v_win_prompt, pool_prompt,
            k_win_sample, v_win_sample, pool_sample)
```

```python
import functools
import math

import jax
import jax.numpy as jnp
from jax import lax
from jax.experimental import pallas as pl
from jax.experimental.pallas import tpu as pltpu

F32 = jnp.float32
BF16 = jnp.bfloat16

HEAD_DIM = 64
DILATED_BRANCHES = ((128, 1), (512, 4), (2048, 16))
NUM_BUCKETS = 32
MAX_DISTANCE = 2048
POOL_WINDOWS = (2, 4, 8, 16)
PAST_LEN = 8192
POOL_CTX = max(POOL_WINDOWS) - 1
POOL_PAD = 16
POOL_LEAD = max(POOL_WINDOWS) // 2
N_ADA = 6
EPS = 1e-6
NEG_INF = -1e30
LOG2E = math.log2(math.e)

LANES = 128
SEG_W = 256
VMEM_LIMIT = 56 * 1024 * 1024
VMEM_LIMIT_BIG = 62 * 1024 * 1024
RING_DEPTH = 3


def _rel_bucket(dist):
    exact = NUM_BUCKETS // 2
    d = jnp.maximum(dist.astype(F32), 1.0)
    large = exact + (jnp.log(d / exact) / math.log(MAX_DISTANCE / exact)
                     * (NUM_BUCKETS - exact)).astype(jnp.int32)
    large = jnp.minimum(large, NUM_BUCKETS - 1)
    return jnp.where(dist < exact, dist, large)


def _bias_from_buckets(bkt, relb_ref, head):
    out = jnp.full(bkt.shape, NEG_INF, F32)
    for b in range(NUM_BUCKETS):
        out = jnp.where(bkt == b, relb_ref[b, head], out)
    return out


def _adaln_kernel(c_ref, w_ref, b_ref, o_ref):
    c = c_ref[...]
    s = c / (1.0 + jnp.exp(-c))
    o_ref[...] = jnp.dot(s, w_ref[...], preferred_element_type=F32,
                         precision=lax.Precision.HIGHEST) + b_ref[...]


def _adaln(c, w, b, tn=1024):
    m, d = c.shape
    n = w.shape[1]
    return pl.pallas_call(
        _adaln_kernel,
        out_shape=jax.ShapeDtypeStruct((m, n), F32),
        grid=(n // tn,),
        in_specs=[pl.BlockSpec((m, d), lambda j: (0, 0)),
                  pl.BlockSpec((d, tn), lambda j: (0, j)),
                  pl.BlockSpec((1, tn), lambda j: (0, j))],
        out_specs=pl.BlockSpec((m, tn), lambda j: (0, j)),
        compiler_params=pltpu.CompilerParams(dimension_semantics=("arbitrary",),
                                             vmem_limit_bytes=VMEM_LIMIT),
        name="adaln",
    )(c, w, b.reshape(1, n))


def _modulated_norm(x, g, shift, scale, groups):
    tm, d = x.shape
    ms = jnp.mean(x * x, axis=-1, keepdims=True)
    y = x * lax.rsqrt(ms + EPS) * g
    if groups == 1:
        return y * (1.0 + scale) + shift
    y3 = y.reshape(groups, tm // groups, d)
    return (y3 * (1.0 + scale[:, None, :]) + shift[:, None, :]).reshape(tm, d)


def _per_group(v, x, groups):
    tm, d = x.shape
    if groups == 1:
        return v * x
    return (v[:, None, :] * x.reshape(groups, tm // groups, d)).reshape(tm, d)


def _pool_mix(u3, ctx, wp_ref, ps_ref, buf_a, buf_b, pos):
    nb, t, p = u3.shape
    grp = p // len(POOL_WINDOWS)
    n = POOL_PAD + t
    ctx0 = POOL_LEAD + POOL_PAD - POOL_CTX
    buf_a[:, 0:ctx0, :] = jnp.zeros((nb, ctx0, p), F32)
    buf_b[:, 0:POOL_LEAD, :] = jnp.zeros((nb, POOL_LEAD, p), F32)
    buf_a[:, ctx0:ctx0 + POOL_CTX, :] = ctx
    buf_a[:, ctx0 + POOL_CTX:, :] = u3
    last = buf_a[:, POOL_LEAD + n - POOL_PAD:POOL_LEAD + n, :]

    src, dst = buf_a, buf_b
    where = []
    for g, w in enumerate(POOL_WINDOWS):
        assert w == 2 ** (g + 1)
        cols = slice(g * grp, p)
        dst[:, POOL_LEAD:POOL_LEAD + n, cols] = (src[:, POOL_LEAD:POOL_LEAD + n, cols]
                                                  + src[:, POOL_LEAD - w // 2:POOL_LEAD - w // 2 + n, cols])
        where.append(dst)
        src, dst = dst, src

    new0 = POOL_LEAD + POOL_PAD
    outs = []
    for g, w in enumerate(POOL_WINDOWS):
        cols = slice(g * grp, (g + 1) * grp)
        win = where[g][:, new0:new0 + t, cols]
        cnt = jnp.minimum(float(w), pos + 1.0)
        pooled = (win / cnt - u3[:, :, cols]).reshape(nb * t, grp)
        y = jnp.dot(pooled.astype(BF16), wp_ref[g], preferred_element_type=F32)
        outs.append((y * ps_ref[:, cols]).reshape(nb, t, grp))
    return jnp.concatenate(outs, axis=-1), last


def _mixer_in_kernel(x_ref, shift_ref, scale_ref, g1_ref, w_ref, qg_ref, kg_ref, seg_ref, wp_ref, ps_ref,
                     *refs, groups, tiles_per_seq, pos0, feature_major):
    refs = list(refs)
    ctx_ref = None if tiles_per_seq else refs.pop(0)
    q_ref, k_ref, v_ref, pool_ref, last_ref = refs[:5]
    refs = refs[5:]
    if feature_major:
        kt_ref, vt_ref = refs[:2]
        refs = refs[2:]
    buf_a, buf_b = refs[:2]
    tm, a = q_ref.shape
    p = pool_ref.shape[1]
    t = tm // groups
    row = lax.broadcasted_iota(jnp.int32, (1, t, 1), 1)
    if tiles_per_seq:
        carry = refs[2]
        tile = pl.program_id(0) % tiles_per_seq

        @pl.when(tile == 0)
        def _():
            carry[...] = jnp.zeros(carry.shape, F32)

        ctx = carry[:, POOL_PAD - POOL_CTX:, :]
        pos = (pos0 + tile * tm + row).astype(F32)
    else:
        ctx = ctx_ref[...]
        pos = (pos0 + row).astype(F32)

    h = _modulated_norm(x_ref[...], g1_ref[...], shift_ref[...], scale_ref[...], groups)
    hb = h.astype(BF16)

    def head_norm(z, g):
        parts = []
        for c in range(a // SEG_W):
            zc = z[:, c * SEG_W:(c + 1) * SEG_W]
            ms = jnp.dot((zc * zc).astype(BF16), seg_ref[...], preferred_element_type=F32)
            parts.append(zc * lax.rsqrt(ms + EPS))
        return jnp.concatenate(parts, axis=-1) * g

    zq = jnp.dot(hb, w_ref[:, 0:a], preferred_element_type=F32)
    q_ref[...] = head_norm(zq, qg_ref[...]) * (HEAD_DIM ** -0.5)
    zk = jnp.dot(hb, w_ref[:, a:2 * a], preferred_element_type=F32)
    k = head_norm(zk, kg_ref[...])
    k_ref[...] = k
    v = jnp.dot(hb, w_ref[:, 2 * a:3 * a], preferred_element_type=F32)
    v_ref[...] = v
    if feature_major:
        kt_ref[...] = k.T
        vt_ref[...] = v.T

    u = jnp.dot(hb, w_ref[:, 3 * a:], preferred_element_type=F32)
    pool, last = _pool_mix(u.reshape(groups, t, p), ctx, wp_ref, ps_ref, buf_a, buf_b, pos)
    pool_ref[...] = pool.reshape(tm, p).astype(pool_ref.dtype)
    last_ref[...] = last
    if tiles_per_seq:
        carry[...] = last


def _mixer_in(x2, shift, scale, g1, w_in_b, qg, kg, seg, w_pool_b, pool_scale, ctx, *, tm, rows_per_mod, pos0,
              feature_major=False):
    rows, d = x2.shape
    d_in = w_in_b.shape[1]
    a = qg.shape[1]
    p = d_in - 3 * a
    n_seq = rows // rows_per_mod
    g = len(POOL_WINDOWS)
    const = lambda i: (0, 0)
    row_spec = lambda w: pl.BlockSpec((tm, w), lambda i: (i, 0))
    in_specs = [row_spec(d), None, None,
                pl.BlockSpec((1, d), const),
                pl.BlockSpec((d, d_in), const),
                pl.BlockSpec((1, a), const), pl.BlockSpec((1, a), const),
                pl.BlockSpec((SEG_W, SEG_W), const),
                pl.BlockSpec((g, p // g, p // g), lambda i: (0, 0, 0)),
                pl.BlockSpec((1, p), const)]
    args = [x2, shift, scale, g1, w_in_b, qg, kg, seg, w_pool_b, pool_scale]
    if rows_per_mod >= tm:
        assert ctx is None and rows_per_mod % tm == 0
        groups = 1
        per = rows_per_mod // tm
        args[1] = shift.reshape(-1, 1, d)
        args[2] = scale.reshape(-1, 1, d)
        in_specs[1] = in_specs[2] = pl.BlockSpec((None, 1, d), lambda i: (i // per, 0, 0))
        last_spec = pl.BlockSpec((1, POOL_PAD, p), lambda i: (i // per, 0, 0))
        scratch = [pltpu.VMEM((1, POOL_PAD, p), F32)]
    else:
        assert ctx is not None and not feature_major and tm % rows_per_mod == 0
        groups = tm // rows_per_mod
        per = None
        in_specs[1] = in_specs[2] = pl.BlockSpec((groups, d), lambda i: (i, 0))
        in_specs.append(pl.BlockSpec((groups, POOL_CTX, p), lambda i: (i, 0, 0)))
        args.append(ctx)
        last_spec = pl.BlockSpec((groups, POOL_PAD, p), lambda i: (i, 0, 0))
        scratch = []
    out_shape = ([jax.ShapeDtypeStruct((rows, a), F32)] * 3
                 + [jax.ShapeDtypeStruct((rows, p), BF16), jax.ShapeDtypeStruct((n_seq, POOL_PAD, p), F32)])
    out_specs = [row_spec(a)] * 3 + [row_spec(p), last_spec]
    if feature_major:
        out_shape += [jax.ShapeDtypeStruct((n_seq, a, rows_per_mod), F32)] * 2
        out_specs += [pl.BlockSpec((None, a, tm), lambda i: (i // per, 0, i % per))] * 2
    pool_buf = pltpu.VMEM((groups, POOL_LEAD + POOL_PAD + tm // groups, p), F32)
    return pl.pallas_call(
        functools.partial(_mixer_in_kernel, groups=groups, tiles_per_seq=per, pos0=pos0,
                          feature_major=feature_major),
        out_shape=out_shape,
        grid=(rows // tm,),
        in_specs=in_specs,
        out_specs=out_specs,
        scratch_shapes=[pool_buf, pool_buf] + scratch,
        compiler_params=pltpu.CompilerParams(dimension_semantics=("arbitrary",),
                                             vmem_limit_bytes=VMEM_LIMIT),
        name="mixer_in",
    )(*args)


def _prompt_attn_kernel(relb_ref, bkt_ref, q_ref, k_ref, v_ref, o_ref,
                        qd, kd, vd, bias_s, acc_s, m_s, l_s, *, blk, group):
    s_len = q_ref.shape[0]
    nbr = len(DILATED_BRANCHES)
    mid = DILATED_BRANCHES[1][1]
    msub = s_len // mid
    pair = pl.program_id(0)
    head0 = lax.broadcasted_iota(jnp.int32, (1, LANES), 1) < HEAD_DIM

    @pl.when(pl.program_id(1) == 0)
    def _():
        for bi in range(nbr):
            for hh in range(2):
                bias_s[bi, hh * blk:(hh + 1) * blk, :] = LOG2E * _bias_from_buckets(bkt_ref[bi], relb_ref,
                                                                                      2 * pair + hh)

    for bi, (_, dil) in enumerate(DILATED_BRANCHES):
        sub = s_len // dil
        for src, dst, scale in ((q_ref, qd, LOG2E), (k_ref, kd, None), (v_ref, vd, None)):
            for r in range(dil):
                rows = src[...] if dil == 1 else src[pl.ds(r, sub, stride=dil), :]
                if scale is not None:
                    rows = rows * scale
                dst[bi, r * sub:(r + 1) * sub, :] = rows.astype(BF16)

    def unit(bi, base, dst, first):
        nk = blk if first else 2 * blk
        kbase = base if first else base - blk
        qb = qd[bi, pl.ds(base, blk), :]
        zero = jnp.zeros_like(qb)
        q2 = jnp.concatenate([jnp.where(head0, qb, zero), jnp.where(head0, zero, qb)], axis=0)
        s = lax.dot_general(q2, kd[bi, pl.ds(kbase, nk), :], (((1,), (1,)), ((), ())),
                            preferred_element_type=F32)
        s = s + (bias_s[bi, :, blk:] if first else bias_s[bi])
        m = jnp.max(s, axis=-1, keepdims=True)
        p = jnp.exp2(s - m)
        l = jnp.sum(p, axis=-1, keepdims=True)
        acc = jnp.dot(p.astype(BF16), vd[bi, pl.ds(kbase, nk), :], preferred_element_type=F32)
        full = (blk, LANES)
        acc_s[bi, dst, :] = jnp.where(head0, acc[:blk], acc[blk:])
        m_s[bi, dst, :] = jnp.where(head0, jnp.broadcast_to(m[:blk], full), jnp.broadcast_to(m[blk:], full))
        l_s[bi, dst, :] = jnp.where(head0, jnp.broadcast_to(l[:blk], full), jnp.broadcast_to(l[blk:], full))

    for bi, (_, dil) in enumerate(DILATED_BRANCHES):
        sub = s_len // dil
        nb = sub // blk
        if dil <= mid:
            def residue(r, carry, bi=bi, sub=sub, nb=nb):
                base = pl.multiple_of(r * sub, blk)
                unit(bi, base, pl.ds(base, blk), True)

                def later(n, c):
                    b2 = pl.multiple_of(r * sub + n * blk, blk)
                    unit(bi, b2, pl.ds(b2, blk), False)
                    return c

                lax.fori_loop(1, nb, later, 0, unroll=min(group, nb - 1))
                return carry

            lax.fori_loop(0, dil, residue, 0, unroll=max(1, min(dil, group // nb)))
        else:
            step = dil // mid
            for r in range(dil):
                for n in range(nb):
                    start = (r % mid) * msub + r // mid + n * blk * step
                    unit(bi, r * sub + n * blk, pl.ds(start, blk, stride=step), n == 0)

    for r in range(mid):
        for c in range(msub // blk):
            res = slice(r * msub + c * blk, r * msub + (c + 1) * blk)
            nat = pl.ds(c * blk * mid + r, blk, stride=mid)
            rows = [nat] + [res] * (nbr - 1)
            m = [m_s[bi, rows[bi], :] for bi in range(nbr)]
            top = functools.reduce(jnp.maximum, m)
            num = jnp.zeros((blk, LANES), F32)
            den = jnp.zeros((blk, LANES), F32)
            for bi in range(nbr):
                w = jnp.exp2(m[bi] - top)
                num = num + w * acc_s[bi, rows[bi], :]
                den = den + w * l_s[bi, rows[bi], :]
            o_ref[nat, :] = num / den


def _prompt_attention(q, k, v, rel_bias, blk=128, group=16):
    b, s_len, a = q.shape
    n_pairs = a // LANES
    nbr = len(DILATED_BRANCHES)
    assert all(w // d == blk and (s_len // d) % blk == 0 for w, d in DILATED_BRANCHES)
    qi = jnp.arange(blk)[:, None]
    ki = jnp.arange(2 * blk)[None, :]
    dist = qi + blk - ki
    ok = (dist >= 0) & (dist <= blk)
    bkt = jnp.stack([jnp.where(ok, _rel_bucket(jnp.clip(dist, 0, blk) * d), -1)
                     for _, d in DILATED_BRANCHES]).astype(jnp.int32)
    seq_spec = pl.BlockSpec((None, s_len, LANES), lambda p, i: (i, 0, p))
    return pl.pallas_call(
        functools.partial(_prompt_attn_kernel, blk=blk, group=group),
        out_shape=jax.ShapeDtypeStruct((b, s_len, a), F32),
        grid=(n_pairs, b),
        in_specs=[pl.BlockSpec(memory_space=pltpu.SMEM),
                  pl.BlockSpec((nbr, blk, 2 * blk), lambda p, i: (0, 0, 0)),
                  seq_spec, seq_spec, seq_spec],
        out_specs=seq_spec,
        scratch_shapes=[pltpu.VMEM((nbr, s_len, LANES), BF16)] * 3
                       + [pltpu.VMEM((nbr, 2 * blk, 2 * blk), F32)]
                       + [pltpu.VMEM((nbr, s_len, LANES), F32)] * 3,
        compiler_params=pltpu.CompilerParams(dimension_semantics=("arbitrary", "arbitrary"),
                                             vmem_limit_bytes=VMEM_LIMIT),
        name="prompt_attn",
    )(rel_bias, bkt, q, k, v)


def _sample_tables(t, wb, nk):
    dist = wb + jnp.arange(t)[:, None] - jnp.arange(nk)[None, :]
    mult = jnp.zeros((t, nk), jnp.int32)
    for w, d in DILATED_BRANCHES:
        mult = mult + ((dist >= 0) & (dist <= w) & (dist % d == 0)).astype(jnp.int32)
    bkt = jnp.where(mult > 0, _rel_bucket(jnp.maximum(dist, 0)), -1).astype(jnp.int32)
    return bkt, mult.astype(F32)


def _sample_slab(bias, mult, q_ref, kn_ref, vn_ref, ck_ref, cv_ref, o_ref, ko_ref, vo_ref, kall, vall,
                 row_chunk):
    t, slab = q_ref.shape
    wb = ck_ref.shape[1]
    nk = kall.shape[1]
    pad = nk - wb
    hps = slab // HEAD_DIM

    def append(new_ref, cache_ref, out_ref, all_ref):
        new_t = jnp.concatenate([new_ref[...], jnp.zeros((pad - t, slab), F32)], axis=0).T
        for c in range(slab // row_chunk):
            rows = slice(c * row_chunk, (c + 1) * row_chunk)
            ext = jnp.concatenate([cache_ref[rows, :], new_t[rows, :]], axis=1)
            all_ref[rows, :] = ext.astype(BF16)
            out_ref[rows, :] = pltpu.roll(ext, nk - t, axis=1)[:, 0:wb]

    append(kn_ref, ck_ref, ko_ref, kall)
    append(vn_ref, cv_ref, vo_ref, vall)

    lane = lax.broadcasted_iota(jnp.int32, (hps * t, slab), 1)
    row = lax.broadcasted_iota(jnp.int32, (hps * t, slab), 0)
    own = (lane // HEAD_DIM) == (row // t)
    qs = jnp.where(own, jnp.concatenate([q_ref[...]] * hps, axis=0), 0.0).astype(BF16)
    s = jnp.dot(qs, kall[...], preferred_element_type=F32) + bias
    m = jnp.max(s, axis=-1, keepdims=True)
    p = jnp.exp(s - m) * jnp.concatenate([mult] * hps, axis=0)
    l = jnp.sum(p, axis=-1, keepdims=True)
    o = lax.dot_general(p.astype(BF16), vall[...], (((1,), (1,)), ((), ())),
                        preferred_element_type=F32) / l
    o = jnp.where(own, o, 0.0)
    res = o[0:t]
    for hh in range(1, hps):
        res = res + o[hh * t:(hh + 1) * t]
    o_ref[...] = res


def _tail_sample_kernel(x_ref, attn_ref, pool_ref, gate1_ref, shift2_ref, scale2_ref, gate2_ref, g2_ref,
                        wout_ref, wup_ref, wdown_ref,
                        relb_ref, bkt_ref, mult_ref, q_ref, kn_ref, vn_ref, ck_ref, cv_ref,
                        y_ref, o_ref, ko_ref, vo_ref,
                        hb_s, acc_s, kall, vall, bias_s, ring_k, ring_v, ring_sem,
                        *, n_slab, row_chunk, ff_chunk):
    i = pl.program_id(0)
    j = pl.program_id(1)
    n_phase = pl.num_programs(1)
    a = attn_ref.shape[1]
    t, slab = q_ref.shape
    hps = slab // HEAD_DIM
    step = i * n_phase + j
    n_steps = pl.num_programs(0) * n_phase

    def window_copies(unit, slot):
        seq = unit // n_slab
        rows = pl.ds(pl.multiple_of((unit % n_slab) * slab, slab), slab)
        return [pltpu.make_async_copy(src.at[seq, rows, :], ring.at[slot], ring_sem.at[slot, which])
                for which, (src, ring) in enumerate(((ck_ref, ring_k), (cv_ref, ring_v)))]

    @pl.when(step == 0)
    def _():
        for u in range(RING_DEPTH - 1):
            for c in window_copies(u, u):
                c.start()

    ahead = step + (RING_DEPTH - 1)

    @pl.when(ahead < n_steps)
    def _():
        for c in window_copies(ahead, ahead % RING_DEPTH):
            c.start()

    @pl.when((i == 0) & (j == 0))
    def _():
        for h in range(n_slab * hps):
            bias_s[h // hps, (h % hps) * t:(h % hps + 1) * t, :] = _bias_from_buckets(bkt_ref[...], relb_ref, h)

    def phase(first):
        slot = step % RING_DEPTH
        for c in window_copies(step, slot):
            c.wait()
        _sample_slab(bias_s[step % n_slab], mult_ref[...], q_ref, kn_ref, vn_ref, ring_k.at[slot],
                     ring_v.at[slot], o_ref, ko_ref, vo_ref, kall, vall, row_chunk)
        if first:
            mix = jnp.dot(attn_ref[...].astype(BF16), wout_ref[0:a, :], preferred_element_type=F32)
            mix = mix + jnp.dot(pool_ref[...].astype(BF16), wout_ref[a:, :], preferred_element_type=F32)
            x1 = x_ref[...] + gate1_ref[...] * mix
            y_ref[...] = x1
            hb = _modulated_norm(x1, g2_ref[...], shift2_ref[...], scale2_ref[...], 1).astype(BF16)
            hb_s[...] = hb
            ff = pl.ds(0, ff_chunk)
        else:
            hb = hb_s[...]
            ff = pl.ds(pl.multiple_of(j * ff_chunk, ff_chunk), ff_chunk)
        act = jnp.maximum(jnp.dot(hb, wup_ref[:, ff], preferred_element_type=F32), 0.0)
        part = jnp.dot((act * act).astype(BF16), wdown_ref[ff, :], preferred_element_type=F32)
        if first:
            acc_s[...] = part
        else:
            acc_s[...] += part

    pl.when(j == 0)(functools.partial(phase, True))
    pl.when(j != 0)(functools.partial(phase, False))

    @pl.when(j == n_phase - 1)
    def _():
        y_ref[...] = y_ref[...] + gate2_ref[...] * acc_s[...]


def _tail_and_sample_attention(x2, attn2, pool2, gate1, shift2, scale2, gate2, g2, w_out_b, w_up_b, w_down_b,
                               q, k_new, v_new, cache_kt, cache_vt, rel_bias,
                               *, tm, rows_per_mod, ff_chunk=1024, slab=256, pad=128, row_chunk=64):
    rows, d = x2.shape
    a = attn2.shape[1]
    p = pool2.shape[1]
    d_ff = w_up_b.shape[1]
    db, t, _ = q.shape
    wb = cache_kt.shape[2]
    nk = wb + pad
    n_slab = a // slab
    n_phase = d_ff // ff_chunk
    n_tiles = rows // tm
    hps = slab // HEAD_DIM
    assert pad >= t and wb >= max(w for w, _ in DILATED_BRANCHES)
    assert n_tiles * n_phase == db * n_slab and rows_per_mod % tm == 0
    per = rows_per_mod // tm
    bkt, mult = _sample_tables(t, wb, nk)
    mods = [m.reshape(-1, 1, d) for m in (gate1, shift2, scale2, gate2)]

    row_spec = lambda w: pl.BlockSpec((tm, w), lambda i, j: (i, 0))
    mod_spec = pl.BlockSpec((None, 1, d), lambda i, j: (i // per, 0, 0))
    resident = lambda shape: pl.BlockSpec(shape, lambda i, j: (0,) * len(shape), pipeline_mode=pl.Buffered(1))
    seq = lambda i, j: (i * n_phase + j) // n_slab
    sl = lambda i, j: (i * n_phase + j) % n_slab
    new_spec = pl.BlockSpec((None, t, slab), lambda i, j: (seq(i, j), 0, sl(i, j)))
    win_spec = pl.BlockSpec((None, slab, wb), lambda i, j: (seq(i, j), sl(i, j), 0))
    tab_spec = pl.BlockSpec((t, nk), lambda i, j: (0, 0))
    return pl.pallas_call(
        functools.partial(_tail_sample_kernel, n_slab=n_slab, row_chunk=row_chunk, ff_chunk=ff_chunk),
        out_shape=[jax.ShapeDtypeStruct((rows, d), F32),
                   jax.ShapeDtypeStruct((db, t, a), F32),
                   jax.ShapeDtypeStruct((db, a, wb), F32),
                   jax.ShapeDtypeStruct((db, a, wb), F32)],
        grid=(n_tiles, n_phase),
        in_specs=[row_spec(d), row_spec(a), row_spec(p), mod_spec, mod_spec, mod_spec, mod_spec,
                  pl.BlockSpec((1, d), lambda i, j: (0, 0)),
                  resident((a + p, d)), resident((d, d_ff)), resident((d_ff, d)),
                  pl.BlockSpec(memory_space=pltpu.SMEM), tab_spec, tab_spec,
                  new_spec, new_spec, new_spec,
                  pl.BlockSpec(memory_space=pl.ANY), pl.BlockSpec(memory_space=pl.ANY)],
        out_specs=[row_spec(d), new_spec, win_spec, win_spec],
        scratch_shapes=[pltpu.VMEM((tm, d), BF16), pltpu.VMEM((tm, d), F32),
                        pltpu.VMEM((slab, nk), BF16), pltpu.VMEM((slab, nk), BF16),
                        pltpu.VMEM((n_slab, hps * t, nk), F32),
                        pltpu.VMEM((RING_DEPTH, slab, wb), F32), pltpu.VMEM((RING_DEPTH, slab, wb), F32),
                        pltpu.SemaphoreType.DMA((RING_DEPTH, 2))],
        compiler_params=pltpu.CompilerParams(dimension_semantics=("arbitrary", "arbitrary"),
                                             vmem_limit_bytes=VMEM_LIMIT_BIG),
        name="tail_sample",
    )(x2, attn2, pool2, *mods, g2, w_out_b, w_up_b, w_down_b,
      rel_bias, bkt, mult, q, k_new, v_new, cache_kt, cache_vt)


def _tail_kernel(x_ref, attn_ref, pool_ref, gate1_ref, shift2_ref, scale2_ref, gate2_ref, g2_ref,
                 wout_ref, wup_ref, wdown_ref, y_ref, *, groups, ff_chunk):
    a = attn_ref.shape[1]
    d_ff = wup_ref.shape[1]
    mix = jnp.dot(attn_ref[...].astype(BF16), wout_ref[0:a, :], preferred_element_type=F32)
    mix = mix + jnp.dot(pool_ref[...].astype(BF16), wout_ref[a:, :], preferred_element_type=F32)
    x1 = x_ref[...] + _per_group(gate1_ref[...], mix, groups)
    hb = _modulated_norm(x1, g2_ref[...], shift2_ref[...], scale2_ref[...], groups).astype(BF16)
    f = jnp.zeros(x1.shape, F32)
    for c in range(d_ff // ff_chunk):
        cols = slice(c * ff_chunk, (c + 1) * ff_chunk)
        act = jnp.maximum(jnp.dot(hb, wup_ref[:, cols], preferred_element_type=F32), 0.0)
        f = f + jnp.dot((act * act).astype(BF16), wdown_ref[cols, :], preferred_element_type=F32)
    y_ref[...] = x1 + _per_group(gate2_ref[...], f, groups)


def _layer_tail(x2, attn2, pool2, gate1, shift2, scale2, gate2, g2, w_out_b, w_up_b, w_down_b,
                *, tm, rows_per_mod, ff_chunk=1024):
    rows, d = x2.shape
    a = attn2.shape[1]
    p = pool2.shape[1]
    d_ff = w_up_b.shape[1]
    mods = [gate1, shift2, scale2, gate2]
    if rows_per_mod >= tm:
        groups = 1
        per = rows_per_mod // tm
        mods = [m.reshape(-1, 1, d) for m in mods]
        mod_spec = pl.BlockSpec((None, 1, d), lambda i: (i // per, 0, 0))
    else:
        groups = tm // rows_per_mod
        mod_spec = pl.BlockSpec((groups, d), lambda i: (i, 0))
    const = lambda i: (0, 0)
    row_spec = lambda w: pl.BlockSpec((tm, w), lambda i: (i, 0))
    resident = lambda shape: pl.BlockSpec(shape, const, pipeline_mode=pl.Buffered(1))
    return pl.pallas_call(
        functools.partial(_tail_kernel, groups=groups, ff_chunk=ff_chunk),
        out_shape=jax.ShapeDtypeStruct((rows, d), F32),
        grid=(rows // tm,),
        in_specs=[row_spec(d), row_spec(a), row_spec(p), mod_spec, mod_spec, mod_spec, mod_spec,
                  pl.BlockSpec((1, d), const),
                  resident((a + p, d)), resident((d, d_ff)), resident((d_ff, d))],
        out_specs=row_spec(d),
        compiler_params=pltpu.CompilerParams(dimension_semantics=("arbitrary",),
                                             vmem_limit_bytes=VMEM_LIMIT),
        name="layer_tail",
    )(x2, attn2, pool2, *mods, g2, w_out_b, w_up_b, w_down_b)


def kernel(x_prompt, x_sample, c_prompt, c_sample, cache_k, cache_v, state_pool, w_ada, b_ada, norm1_g, norm2_g, w_in, q_norm_g, k_norm_g, rel_bias, w_pool, pool_scale, w_out, w_up, w_down):
    b, s_len, d = x_prompt.shape
    db, t, _ = x_sample.shape
    depth = w_ada.shape[0]
    assert depth == 1, "single-layer step"
    n_heads = rel_bias.shape[1]
    a = n_heads * HEAD_DIM
    wb = cache_k.shape[2]
    l = 0

    ada = _adaln(jnp.concatenate([c_prompt, c_sample], axis=0), w_ada[l], b_ada[l])
    mods = [ada[:, i * d:(i + 1) * d] for i in range(N_ADA)]
    mp = [m[:b] for m in mods]
    msm = [m[b:] for m in mods]

    w_in_b = w_in[l].astype(BF16)
    w_out_b = w_out[l].astype(BF16)
    w_up_b = w_up[l].astype(BF16)
    w_down_b = w_down[l].astype(BF16)
    w_pool_b = w_pool[l].astype(BF16)
    g1 = norm1_g[l].reshape(1, d)
    g2 = norm2_g[l].reshape(1, d)
    qg = jnp.tile(q_norm_g[l], n_heads).reshape(1, a)
    kg = jnp.tile(k_norm_g[l], n_heads).reshape(1, a)
    ps = pool_scale[l].reshape(1, -1)
    seg_i = jnp.arange(SEG_W) // HEAD_DIM
    seg = jnp.where(seg_i[:, None] == seg_i[None, :], 1.0 / HEAD_DIM, 0.0).astype(BF16)

    xp2 = x_prompt.reshape(b * s_len, d)
    q, k, v, pool, u_last, kt, vt = _mixer_in(xp2, mp[0], mp[1], g1, w_in_b, qg, kg, seg, w_pool_b, ps, None,
                                              tm=512, rows_per_mod=s_len, pos0=0, feature_major=True)
    attn = _prompt_attention(q.reshape(b, s_len, a), k.reshape(b, s_len, a), v.reshape(b, s_len, a), rel_bias)
    keep = min(max(w for w, _ in DILATED_BRANCHES), s_len)
    win = lambda c: jnp.transpose(c.reshape(b, n_heads, HEAD_DIM, s_len), (0, 3, 1, 2))[None, :, s_len - keep:]
    k_win_prompt = win(kt)
    v_win_prompt = win(vt)
    pool_prompt = u_last[None, :, POOL_PAD - POOL_CTX:]

    xs2 = x_sample.reshape(db * t, d)
    qs, ks, vs, pool_s, us_last = _mixer_in(xs2, msm[0], msm[1], g1, w_in_b, qg, kg, seg, w_pool_b, ps,
                                            state_pool[l], tm=256, rows_per_mod=t, pos0=PAST_LEN)
    to_fm = lambda c: jnp.transpose(c, (0, 2, 3, 1)).reshape(db, a, wb)
    from_fm = lambda c: jnp.transpose(c.reshape(db, n_heads, HEAD_DIM, wb), (0, 3, 1, 2))[None]
    y_prompt, attn_s, k_win, v_win = _tail_and_sample_attention(
        xp2, attn.reshape(b * s_len, a), pool,
        mp[2], mp[3], mp[4], mp[5], g2, w_out_b, w_up_b, w_down_b,
        qs.reshape(db, t, a), ks.reshape(db, t, a), vs.reshape(db, t, a),
        to_fm(cache_k[l]), to_fm(cache_v[l]), rel_bias, tm=512, rows_per_mod=s_len)
    y_prompt = y_prompt.reshape(b, s_len, d)
    y_sample = _layer_tail(xs2, attn_s.reshape(db * t, a), pool_s,
                           msm[2], msm[3], msm[4], msm[5], g2, w_out_b, w_up_b, w_down_b,
                           tm=256, rows_per_mod=t).reshape(db, t, d)
    k_win_sample = from_fm(k_win)
    v_win_sample = from_fm(v_win)
    pool_sample = us_last[None, :, POOL_PAD - POOL_CTX:]

    return (y_prompt, y_sample, k_win_prompt, v_win_prompt, pool_prompt,
            k_win_sample, v_win_sample, pool_sample)
```

```python
import functools
import math

import jax
import jax.numpy as jnp
from jax import lax
from jax.experimental import pallas as pl
from jax.experimental.pallas import tpu as pltpu

F32 = jnp.float32
BF16 = jnp.bfloat16

HEAD_DIM = 64
DILATED_BRANCHES = ((128, 1), (512, 4), (2048, 16))
NUM_BUCKETS = 32
MAX_DISTANCE = 2048
POOL_WINDOWS = (2, 4, 8, 16)
PAST_LEN = 8192
POOL_CTX = max(POOL_WINDOWS) - 1
POOL_PAD = 16
POOL_LEAD = max(POOL_WINDOWS) // 2
N_ADA = 6
EPS = 1e-6
NEG_INF = -1e30
LOG2E = math.log2(math.e)

LANES = 128
SEG_W = 256
VMEM_LIMIT = 56 * 1024 * 1024
VMEM_LIMIT_BIG = 62 * 1024 * 1024
RING_DEPTH = 3


def _rel_bucket(dist):
    exact = NUM_BUCKETS // 2
    d = jnp.maximum(dist.astype(F32), 1.0)
    large = exact + (jnp.log(d / exact) / math.log(MAX_DISTANCE / exact)
                     * (NUM_BUCKETS - exact)).astype(jnp.int32)
    large = jnp.minimum(large, NUM_BUCKETS - 1)
    return jnp.where(dist < exact, dist, large)


def _bias_from_buckets(bkt, relb_ref, head):
    out = jnp.full(bkt.shape, NEG_INF, F32)
    for b in range(NUM_BUCKETS):
        out = jnp.where(bkt == b, relb_ref[b, head], out)
    return out


def _adaln_kernel(c_ref, w_ref, b_ref, o_ref):
    c = c_ref[...]
    s = c / (1.0 + jnp.exp(-c))
    o_ref[...] = jnp.dot(s.astype(BF16), w_ref[...].astype(BF16), preferred_element_type=F32) + b_ref[...]


def _adaln(c, w, b, tn=1024):
    m, d = c.shape
    n = w.shape[1]
    return pl.pallas_call(
        _adaln_kernel,
        out_shape=jax.ShapeDtypeStruct((m, n), F32),
        grid=(n // tn,),
        in_specs=[pl.BlockSpec((m, d), lambda j: (0, 0)),
                  pl.BlockSpec((d, tn), lambda j: (0, j)),
                  pl.BlockSpec((1, tn), lambda j: (0, j))],
        out_specs=pl.BlockSpec((m, tn), lambda j: (0, j)),
        compiler_params=pltpu.CompilerParams(dimension_semantics=("arbitrary",),
                                             vmem_limit_bytes=VMEM_LIMIT),
        name="adaln",
    )(c, w, b.reshape(1, n))


def _modulated_norm(x, g, shift, scale, groups):
    tm, d = x.shape
    ms = jnp.mean(x * x, axis=-1, keepdims=True)
    y = x * lax.rsqrt(ms + EPS) * g
    if groups == 1:
        return y * (1.0 + scale) + shift
    y3 = y.reshape(groups, tm // groups, d)
    return (y3 * (1.0 + scale[:, None, :]) + shift[:, None, :]).reshape(tm, d)


def _per_group(v, x, groups):
    tm, d = x.shape
    if groups == 1:
        return v * x
    return (v[:, None, :] * x.reshape(groups, tm // groups, d)).reshape(tm, d)


def _pool_mix(u3, ctx, wp_ref, ps_ref, buf_a, buf_b, pos):
    nb, t, p = u3.shape
    grp = p // len(POOL_WINDOWS)
    n = POOL_PAD + t
    ctx0 = POOL_LEAD + POOL_PAD - POOL_CTX
    buf_a[:, 0:ctx0, :] = jnp.zeros((nb, ctx0, p), F32)
    buf_b[:, 0:POOL_LEAD, :] = jnp.zeros((nb, POOL_LEAD, p), F32)
    buf_a[:, ctx0:ctx0 + POOL_CTX, :] = ctx
    buf_a[:, ctx0 + POOL_CTX:, :] = u3
    last = buf_a[:, POOL_LEAD + n - POOL_PAD:POOL_LEAD + n, :]

    src, dst = buf_a, buf_b
    where = []
    for g, w in enumerate(POOL_WINDOWS):
        assert w == 2 ** (g + 1)
        cols = slice(g * grp, p)
        dst[:, POOL_LEAD:POOL_LEAD + n, cols] = (src[:, POOL_LEAD:POOL_LEAD + n, cols]
                                                  + src[:, POOL_LEAD - w // 2:POOL_LEAD - w // 2 + n, cols])
        where.append(dst)
        src, dst = dst, src

    new0 = POOL_LEAD + POOL_PAD
    outs = []
    for g, w in enumerate(POOL_WINDOWS):
        cols = slice(g * grp, (g + 1) * grp)
        win = where[g][:, new0:new0 + t, cols]
        cnt = jnp.minimum(float(w), pos + 1.0)
        pooled = (win / cnt - u3[:, :, cols]).reshape(nb * t, grp)
        y = jnp.dot(pooled.astype(BF16), wp_ref[g], preferred_element_type=F32)
        outs.append((y * ps_ref[:, cols]).reshape(nb, t, grp))
    return jnp.concatenate(outs, axis=-1), last


def _mixer_in_kernel(x_ref, shift_ref, scale_ref, g1_ref, w_ref, qg_ref, kg_ref, seg_ref, wp_ref, ps_ref,
                     *refs, groups, tiles_per_seq, pos0, feature_major):
    refs = list(refs)
    ctx_ref = None if tiles_per_seq else refs.pop(0)
    q_ref, k_ref, v_ref, pool_ref, last_ref = refs[:5]
    refs = refs[5:]
    if feature_major:
        kt_ref, vt_ref = refs[:2]
        refs = refs[2:]
    buf_a, buf_b = refs[:2]
    tm, a = q_ref.shape
    p = pool_ref.shape[1]
    t = tm // groups
    row = lax.broadcasted_iota(jnp.int32, (1, t, 1), 1)
    if tiles_per_seq:
        carry = refs[2]
        tile = pl.program_id(0) % tiles_per_seq

        @pl.when(tile == 0)
        def _():
            carry[...] = jnp.zeros(carry.shape, F32)

        ctx = carry[:, POOL_PAD - POOL_CTX:, :]
        pos = (pos0 + tile * tm + row).astype(F32)
    else:
        ctx = ctx_ref[...]
        pos = (pos0 + row).astype(F32)

    h = _modulated_norm(x_ref[...], g1_ref[...], shift_ref[...], scale_ref[...], groups)
    hb = h.astype(BF16)

    def head_norm(z, g):
        parts = []
        for c in range(a // SEG_W):
            zc = z[:, c * SEG_W:(c + 1) * SEG_W]
            ms = jnp.dot((zc * zc).astype(BF16), seg_ref[...], preferred_element_type=F32)
            parts.append(zc * lax.rsqrt(ms + EPS))
        return jnp.concatenate(parts, axis=-1) * g

    zq = jnp.dot(hb, w_ref[:, 0:a], preferred_element_type=F32)
    q_ref[...] = head_norm(zq, qg_ref[...]) * (HEAD_DIM ** -0.5)
    zk = jnp.dot(hb, w_ref[:, a:2 * a], preferred_element_type=F32)
    k = head_norm(zk, kg_ref[...])
    k_ref[...] = k
    v = jnp.dot(hb, w_ref[:, 2 * a:3 * a], preferred_element_type=F32)
    v_ref[...] = v
    if feature_major:
        kt_ref[...] = k.T
        vt_ref[...] = v.T

    u = jnp.dot(hb, w_ref[:, 3 * a:], preferred_element_type=F32)
    pool, last = _pool_mix(u.reshape(groups, t, p), ctx, wp_ref, ps_ref, buf_a, buf_b, pos)
    pool_ref[...] = pool.reshape(tm, p).astype(pool_ref.dtype)
    last_ref[...] = last
    if tiles_per_seq:
        carry[...] = last


def _mixer_in(x2, shift, scale, g1, w_in_b, qg, kg, seg, w_pool_b, pool_scale, ctx, *, tm, rows_per_mod, pos0,
              feature_major=False):
    rows, d = x2.shape
    d_in = w_in_b.shape[1]
    a = qg.shape[1]
    p = d_in - 3 * a
    n_seq = rows // rows_per_mod
    g = len(POOL_WINDOWS)
    const = lambda i: (0, 0)
    row_spec = lambda w: pl.BlockSpec((tm, w), lambda i: (i, 0))
    in_specs = [row_spec(d), None, None,
                pl.BlockSpec((1, d), const),
                pl.BlockSpec((d, d_in), const),
                pl.BlockSpec((1, a), const), pl.BlockSpec((1, a), const),
                pl.BlockSpec((SEG_W, SEG_W), const),
                pl.BlockSpec((g, p // g, p // g), lambda i: (0, 0, 0)),
                pl.BlockSpec((1, p), const)]
    args = [x2, shift, scale, g1, w_in_b, qg, kg, seg, w_pool_b, pool_scale]
    if rows_per_mod >= tm:
        assert ctx is None and rows_per_mod % tm == 0
        groups = 1
        per = rows_per_mod // tm
        args[1] = shift.reshape(-1, 1, d)
        args[2] = scale.reshape(-1, 1, d)
        in_specs[1] = in_specs[2] = pl.BlockSpec((None, 1, d), lambda i: (i // per, 0, 0))
        last_spec = pl.BlockSpec((1, POOL_PAD, p), lambda i: (i // per, 0, 0))
        scratch = [pltpu.VMEM((1, POOL_PAD, p), F32)]
    else:
        assert ctx is not None and not feature_major and tm % rows_per_mod == 0
        groups = tm // rows_per_mod
        per = None
        in_specs[1] = in_specs[2] = pl.BlockSpec((groups, d), lambda i: (i, 0))
        in_specs.append(pl.BlockSpec((groups, POOL_CTX, p), lambda i: (i, 0, 0)))
        args.append(ctx)
        last_spec = pl.BlockSpec((groups, POOL_PAD, p), lambda i: (i, 0, 0))
        scratch = []
    out_shape = ([jax.ShapeDtypeStruct((rows, a), F32)] * 3
                 + [jax.ShapeDtypeStruct((rows, p), BF16), jax.ShapeDtypeStruct((n_seq, POOL_PAD, p), F32)])
    out_specs = [row_spec(a)] * 3 + [row_spec(p), last_spec]
    if feature_major:
        out_shape += [jax.ShapeDtypeStruct((n_seq, a, rows_per_mod), F32)] * 2
        out_specs += [pl.BlockSpec((None, a, tm), lambda i: (i // per, 0, i % per))] * 2
    pool_buf = pltpu.VMEM((groups, POOL_LEAD + POOL_PAD + tm // groups, p), F32)
    return pl.pallas_call(
        functools.partial(_mixer_in_kernel, groups=groups, tiles_per_seq=per, pos0=pos0,
                          feature_major=feature_major),
        out_shape=out_shape,
        grid=(rows // tm,),
        in_specs=in_specs,
        out_specs=out_specs,
        scratch_shapes=[pool_buf, pool_buf] + scratch,
        compiler_params=pltpu.CompilerParams(dimension_semantics=("arbitrary",),
                                             vmem_limit_bytes=VMEM_LIMIT),
        name="mixer_in",
    )(*args)


def _prompt_attn_kernel(relb_ref, bkt_ref, q_ref, k_ref, v_ref, o_ref,
                        qd, kd, vd, bias_s, acc_s, m_s, l_s, mid_s, out_s, *, blk, group):
    s_len = q_ref.shape[0]
    nbr = len(DILATED_BRANCHES)
    mid = DILATED_BRANCHES[1][1]
    msub = s_len // mid
    pair = pl.program_id(0)
    head0 = lax.broadcasted_iota(jnp.int32, (1, LANES), 1) < HEAD_DIM

    @pl.when(pl.program_id(1) == 0)
    def _():
        for bi in range(nbr):
            for hh in range(2):
                bias_s[bi, hh * blk:(hh + 1) * blk, :] = LOG2E * _bias_from_buckets(bkt_ref[bi], relb_ref,
                                                                                      2 * pair + hh)

    for src, dst, scale in ((q_ref, qd, LOG2E), (k_ref, kd, None), (v_ref, vd, None)):
        for bi, (_, dil) in enumerate(DILATED_BRANCHES):
            sub = s_len // dil
            for r in range(dil):
                if dil == 1:
                    rows = src[...]
                elif dil <= mid:
                    rows = src[pl.ds(r, sub, stride=dil), :]
                    if dil == mid:
                        mid_s[r * sub:(r + 1) * sub, :] = rows
                else:
                    rows = mid_s[pl.ds((r % mid) * msub + r // mid, sub, stride=dil // mid), :]
                if scale is not None:
                    rows = rows * scale
                dst[bi, r * sub:(r + 1) * sub, :] = rows.astype(BF16)

    def unit(bi, base, dst, first):
        nk = blk if first else 2 * blk
        kbase = base if first else base - blk
        qb = qd[bi, pl.ds(base, blk), :]
        zero = jnp.zeros_like(qb)
        q2 = jnp.concatenate([jnp.where(head0, qb, zero), jnp.where(head0, zero, qb)], axis=0)
        s = lax.dot_general(q2, kd[bi, pl.ds(kbase, nk), :], (((1,), (1,)), ((), ())),
                            preferred_element_type=F32)
        s = s + (bias_s[bi, :, blk:] if first else bias_s[bi])
        m = jnp.max(s, axis=-1, keepdims=True)
        p = jnp.exp2(s - m)
        l = jnp.sum(p, axis=-1, keepdims=True)
        acc = jnp.dot(p.astype(BF16), vd[bi, pl.ds(kbase, nk), :], preferred_element_type=F32)
        full = (blk, LANES)
        acc_s[bi, dst, :] = jnp.where(head0, acc[:blk], acc[blk:])
        m_s[bi, dst, :] = jnp.where(head0, jnp.broadcast_to(m[:blk], full), jnp.broadcast_to(m[blk:], full))
        l_s[bi, dst, :] = jnp.where(head0, jnp.broadcast_to(l[:blk], full), jnp.broadcast_to(l[blk:], full))

    for bi, (_, dil) in enumerate(DILATED_BRANCHES):
        sub = s_len // dil
        nb = sub // blk
        if dil <= mid:
            def residue(r, carry, bi=bi, sub=sub, nb=nb):
                base = pl.multiple_of(r * sub, blk)
                unit(bi, base, pl.ds(base, blk), True)

                def later(n, c):
                    b2 = pl.multiple_of(r * sub + n * blk, blk)
                    unit(bi, b2, pl.ds(b2, blk), False)
                    return c

                lax.fori_loop(1, nb, later, 0, unroll=min(group, nb - 1))
                return carry

            lax.fori_loop(0, dil, residue, 0, unroll=max(1, min(dil, group // nb)))
        else:
            step = dil // mid
            for r in range(dil):
                for n in range(nb):
                    start = (r % mid) * msub + r // mid + n * blk * step
                    unit(bi, r * sub + n * blk, pl.ds(start, blk, stride=step), n == 0)

    for r in range(mid):
        for c in range(msub // blk):
            res = slice(r * msub + c * blk, r * msub + (c + 1) * blk)
            nat = pl.ds(c * blk * mid + r, blk, stride=mid)
            rows = [nat] + [res] * (nbr - 1)
            m = [m_s[bi, rows[bi], :] for bi in range(nbr)]
            top = functools.reduce(jnp.maximum, m)
            num = jnp.zeros((blk, LANES), F32)
            den = jnp.zeros((blk, LANES), F32)
            for bi in range(nbr):
                w = jnp.exp2(m[bi] - top)
                num = num + w * acc_s[bi, rows[bi], :]
                den = den + w * l_s[bi, rows[bi], :]
            out_s[nat, :] = num / den
    o_ref[...] = out_s[...].astype(o_ref.dtype)


def _prompt_attention(q, k, v, rel_bias, blk=128, group=16):
    b, s_len, a = q.shape
    n_pairs = a // LANES
    nbr = len(DILATED_BRANCHES)
    assert all(w // d == blk and (s_len // d) % blk == 0 for w, d in DILATED_BRANCHES)
    qi = jnp.arange(blk)[:, None]
    ki = jnp.arange(2 * blk)[None, :]
    dist = qi + blk - ki
    ok = (dist >= 0) & (dist <= blk)
    bkt = jnp.stack([jnp.where(ok, _rel_bucket(jnp.clip(dist, 0, blk) * d), -1)
                     for _, d in DILATED_BRANCHES]).astype(jnp.int32)
    seq_spec = pl.BlockSpec((None, s_len, LANES), lambda p, i: (i, 0, p))
    return pl.pallas_call(
        functools.partial(_prompt_attn_kernel, blk=blk, group=group),
        out_shape=jax.ShapeDtypeStruct((b, s_len, a), BF16),
        grid=(n_pairs, b),
        in_specs=[pl.BlockSpec(memory_space=pltpu.SMEM),
                  pl.BlockSpec((nbr, blk, 2 * blk), lambda p, i: (0, 0, 0)),
                  seq_spec, seq_spec, seq_spec],
        out_specs=seq_spec,
        scratch_shapes=[pltpu.VMEM((nbr, s_len, LANES), BF16)] * 3
                       + [pltpu.VMEM((nbr, 2 * blk, 2 * blk), F32)]
                       + [pltpu.VMEM((nbr, s_len, LANES), F32)] * 3
                       + [pltpu.VMEM((s_len, LANES), F32)] * 2,
        compiler_params=pltpu.CompilerParams(dimension_semantics=("arbitrary", "arbitrary"),
                                             vmem_limit_bytes=VMEM_LIMIT),
        name="prompt_attn",
    )(rel_bias, bkt, q, k, v)


def _sample_tables(t, wb, nk):
    dist = wb + jnp.arange(t)[:, None] - jnp.arange(nk)[None, :]
    mult = jnp.zeros((t, nk), jnp.int32)
    for w, d in DILATED_BRANCHES:
        mult = mult + ((dist >= 0) & (dist <= w) & (dist % d == 0)).astype(jnp.int32)
    bkt = jnp.where(mult > 0, _rel_bucket(jnp.maximum(dist, 0)), -1).astype(jnp.int32)
    return bkt, mult.astype(F32)


def _sample_slab(bias, mult, q_ref, kn_ref, vn_ref, ck_ref, cv_ref, o_ref, ko_ref, vo_ref, kall, vall,
                 row_chunk):
    t, slab = q_ref.shape
    wb = ck_ref.shape[1]
    nk = kall.shape[1]
    pad = nk - wb
    hps = slab // HEAD_DIM

    def append(new_ref, cache_ref, out_ref, all_ref):
        new_t = jnp.concatenate([new_ref[...], jnp.zeros((pad - t, slab), F32)], axis=0).T
        for c in range(slab // row_chunk):
            rows = slice(c * row_chunk, (c + 1) * row_chunk)
            ext = jnp.concatenate([cache_ref[rows, :], new_t[rows, :]], axis=1)
            all_ref[rows, :] = ext.astype(BF16)
            out_ref[rows, :] = pltpu.roll(ext, nk - t, axis=1)[:, 0:wb]

    append(kn_ref, ck_ref, ko_ref, kall)
    append(vn_ref, cv_ref, vo_ref, vall)

    lane = lax.broadcasted_iota(jnp.int32, (hps * t, slab), 1)
    row = lax.broadcasted_iota(jnp.int32, (hps * t, slab), 0)
    own = (lane // HEAD_DIM) == (row // t)
    qs = jnp.where(own, jnp.concatenate([q_ref[...]] * hps, axis=0), 0.0).astype(BF16)
    s = jnp.dot(qs, kall[...], preferred_element_type=F32) + bias
    m = jnp.max(s, axis=-1, keepdims=True)
    p = jnp.exp(s - m) * jnp.concatenate([mult] * hps, axis=0)
    l = jnp.sum(p, axis=-1, keepdims=True)
    o = lax.dot_general(p.astype(BF16), vall[...], (((1,), (1,)), ((), ())),
                        preferred_element_type=F32) / l
    o = jnp.where(own, o, 0.0)
    res = o[0:t]
    for hh in range(1, hps):
        res = res + o[hh * t:(hh + 1) * t]
    o_ref[...] = res


def _tail_sample_kernel(x_ref, attn_ref, pool_ref, gate1_ref, shift2_ref, scale2_ref, gate2_ref, g2_ref,
                        wout_ref, wup_ref, wdown_ref,
                        relb_ref, bkt_ref, mult_ref, q_ref, kn_ref, vn_ref, ck_ref, cv_ref,
                        y_ref, o_ref, ko_ref, vo_ref,
                        hb_s, acc_s, kall, vall, bias_s, ring_k, ring_v, ring_sem,
                        *, n_slab, row_chunk, ff_chunk):
    i = pl.program_id(0)
    j = pl.program_id(1)
    n_phase = pl.num_programs(1)
    a = attn_ref.shape[1]
    t, slab = q_ref.shape
    hps = slab // HEAD_DIM
    step = i * n_phase + j
    n_steps = pl.num_programs(0) * n_phase

    def window_copies(unit, slot):
        seq = unit // n_slab
        rows = pl.ds(pl.multiple_of((unit % n_slab) * slab, slab), slab)
        return [pltpu.make_async_copy(src.at[seq, rows, :], ring.at[slot], ring_sem.at[slot, which])
                for which, (src, ring) in enumerate(((ck_ref, ring_k), (cv_ref, ring_v)))]

    @pl.when(step == 0)
    def _():
        for u in range(RING_DEPTH - 1):
            for c in window_copies(u, u):
                c.start()

    ahead = step + (RING_DEPTH - 1)

    @pl.when(ahead < n_steps)
    def _():
        for c in window_copies(ahead, ahead % RING_DEPTH):
            c.start()

    @pl.when((i == 0) & (j == 0))
    def _():
        for h in range(n_slab * hps):
            bias_s[h // hps, (h % hps) * t:(h % hps + 1) * t, :] = _bias_from_buckets(bkt_ref[...], relb_ref, h)

    def phase(first):
        slot = step % RING_DEPTH
        for c in window_copies(step, slot):
            c.wait()
        _sample_slab(bias_s[step % n_slab], mult_ref[...], q_ref, kn_ref, vn_ref, ring_k.at[slot],
                     ring_v.at[slot], o_ref, ko_ref, vo_ref, kall, vall, row_chunk)
        if first:
            mix = jnp.dot(attn_ref[...].astype(BF16), wout_ref[0:a, :], preferred_element_type=F32)
            mix = mix + jnp.dot(pool_ref[...].astype(BF16), wout_ref[a:, :], preferred_element_type=F32)
            x1 = x_ref[...] + gate1_ref[...] * mix
            y_ref[...] = x1
            hb = _modulated_norm(x1, g2_ref[...], shift2_ref[...], scale2_ref[...], 1).astype(BF16)
            hb_s[...] = hb
            ff = pl.ds(0, ff_chunk)
        else:
            hb = hb_s[...]
            ff = pl.ds(pl.multiple_of(j * ff_chunk, ff_chunk), ff_chunk)
        act = jnp.maximum(jnp.dot(hb, wup_ref[:, ff], preferred_element_type=F32), 0.0)
        part = jnp.dot((act * act).astype(BF16), wdown_ref[ff, :], preferred_element_type=F32)
        if first:
            acc_s[...] = part
        else:
            acc_s[...] += part

    pl.when(j == 0)(functools.partial(phase, True))
    pl.when(j != 0)(functools.partial(phase, False))

    @pl.when(j == n_phase - 1)
    def _():
        y_ref[...] = y_ref[...] + gate2_ref[...] * acc_s[...]


def _tail_and_sample_attention(x2, attn2, pool2, gate1, shift2, scale2, gate2, g2, w_out_b, w_up_b, w_down_b,
                               q, k_new, v_new, cache_kt, cache_vt, rel_bias,
                               *, tm, rows_per_mod, ff_chunk=1024, slab=256, pad=128, row_chunk=64):
    rows, d = x2.shape
    a = attn2.shape[1]
    p = pool2.shape[1]
    d_ff = w_up_b.shape[1]
    db, t, _ = q.shape
    wb = cache_kt.shape[2]
    nk = wb + pad
    n_slab = a // slab
    n_phase = d_ff // ff_chunk
    n_tiles = rows // tm
    hps = slab // HEAD_DIM
    assert pad >= t and wb >= max(w for w, _ in DILATED_BRANCHES)
    assert n_tiles * n_phase == db * n_slab and rows_per_mod % tm == 0
    per = rows_per_mod // tm
    bkt, mult = _sample_tables(t, wb, nk)
    mods = [m.reshape(-1, 1, d) for m in (gate1, shift2, scale2, gate2)]

    row_spec = lambda w: pl.BlockSpec((tm, w), lambda i, j: (i, 0))
    mod_spec = pl.BlockSpec((None, 1, d), lambda i, j: (i // per, 0, 0))
    resident = lambda shape: pl.BlockSpec(shape, lambda i, j: (0,) * len(shape), pipeline_mode=pl.Buffered(1))
    seq = lambda i, j: (i * n_phase + j) // n_slab
    sl = lambda i, j: (i * n_phase + j) % n_slab
    new_spec = pl.BlockSpec((None, t, slab), lambda i, j: (seq(i, j), 0, sl(i, j)))
    win_spec = pl.BlockSpec((None, slab, wb), lambda i, j: (seq(i, j), sl(i, j), 0))
    tab_spec = pl.BlockSpec((t, nk), lambda i, j: (0, 0))
    return pl.pallas_call(
        functools.partial(_tail_sample_kernel, n_slab=n_slab, row_chunk=row_chunk, ff_chunk=ff_chunk),
        out_shape=[jax.ShapeDtypeStruct((rows, d), F32),
                   jax.ShapeDtypeStruct((db, t, a), F32),
                   jax.ShapeDtypeStruct((db, a, wb), F32),
                   jax.ShapeDtypeStruct((db, a, wb), F32)],
        grid=(n_tiles, n_phase),
        in_specs=[row_spec(d), row_spec(a), row_spec(p), mod_spec, mod_spec, mod_spec, mod_spec,
                  pl.BlockSpec((1, d), lambda i, j: (0, 0)),
                  resident((a + p, d)), resident((d, d_ff)), resident((d_ff, d)),
                  pl.BlockSpec(memory_space=pltpu.SMEM), tab_spec, tab_spec,
                  new_spec, new_spec, new_spec,
                  pl.BlockSpec(memory_space=pl.ANY), pl.BlockSpec(memory_space=pl.ANY)],
        out_specs=[row_spec(d), new_spec, win_spec, win_spec],
        scratch_shapes=[pltpu.VMEM((tm, d), BF16), pltpu.VMEM((tm, d), F32),
                        pltpu.VMEM((slab, nk), BF16), pltpu.VMEM((slab, nk), BF16),
                        pltpu.VMEM((n_slab, hps * t, nk), F32),
                        pltpu.VMEM((RING_DEPTH, slab, wb), F32), pltpu.VMEM((RING_DEPTH, slab, wb), F32),
                        pltpu.SemaphoreType.DMA((RING_DEPTH, 2))],
        compiler_params=pltpu.CompilerParams(dimension_semantics=("arbitrary", "arbitrary"),
                                             vmem_limit_bytes=VMEM_LIMIT_BIG),
        name="tail_sample",
    )(x2, attn2, pool2, *mods, g2, w_out_b, w_up_b, w_down_b,
      rel_bias, bkt, mult, q, k_new, v_new, cache_kt, cache_vt)


def _tail_kernel(x_ref, attn_ref, pool_ref, gate1_ref, shift2_ref, scale2_ref, gate2_ref, g2_ref,
                 wout_ref, wup_ref, wdown_ref, y_ref, hb_s, acc_s, *, groups):
    j = pl.program_id(1)
    a = attn_ref.shape[1]

    @pl.when(j == 0)
    def _():
        mix = jnp.dot(attn_ref[...].astype(BF16), wout_ref[0:a, :], preferred_element_type=F32)
        mix = mix + jnp.dot(pool_ref[...].astype(BF16), wout_ref[a:, :], preferred_element_type=F32)
        x1 = x_ref[...] + _per_group(gate1_ref[...], mix, groups)
        y_ref[...] = x1
        hb_s[...] = _modulated_norm(x1, g2_ref[...], shift2_ref[...], scale2_ref[...], groups).astype(BF16)
        acc_s[...] = jnp.zeros(acc_s.shape, F32)

    act = jnp.maximum(jnp.dot(hb_s[...], wup_ref[...], preferred_element_type=F32), 0.0)
    acc_s[...] += jnp.dot((act * act).astype(BF16), wdown_ref[...], preferred_element_type=F32)

    @pl.when(j == pl.num_programs(1) - 1)
    def _():
        y_ref[...] = y_ref[...] + _per_group(gate2_ref[...], acc_s[...], groups)


def _layer_tail(x2, attn2, pool2, gate1, shift2, scale2, gate2, g2, w_out_b, w_up_b, w_down_b,
                *, tm, rows_per_mod, ff_chunk=1024):
    rows, d = x2.shape
    a = attn2.shape[1]
    p = pool2.shape[1]
    d_ff = w_up_b.shape[1]
    mods = [gate1, shift2, scale2, gate2]
    if rows_per_mod >= tm:
        groups = 1
        per = rows_per_mod // tm
        mods = [m.reshape(-1, 1, d) for m in mods]
        mod_spec = pl.BlockSpec((None, 1, d), lambda i, j: (i // per, 0, 0))
    else:
        groups = tm // rows_per_mod
        mod_spec = pl.BlockSpec((groups, d), lambda i, j: (i, 0))
    const = lambda i, j: (0, 0)
    row_spec = lambda w: pl.BlockSpec((tm, w), lambda i, j: (i, 0))
    return pl.pallas_call(
        functools.partial(_tail_kernel, groups=groups),
        out_shape=jax.ShapeDtypeStruct((rows, d), F32),
        grid=(rows // tm, d_ff // ff_chunk),
        in_specs=[row_spec(d), row_spec(a), row_spec(p), mod_spec, mod_spec, mod_spec, mod_spec,
                  pl.BlockSpec((1, d), const),
                  pl.BlockSpec((a + p, d), const),
                  pl.BlockSpec((d, ff_chunk), lambda i, j: (0, j)),
                  pl.BlockSpec((ff_chunk, d), lambda i, j: (j, 0))],
        out_specs=row_spec(d),
        scratch_shapes=[pltpu.VMEM((tm, d), BF16), pltpu.VMEM((tm, d), F32)],
        compiler_params=pltpu.CompilerParams(dimension_semantics=("arbitrary", "arbitrary"),
                                             vmem_limit_bytes=VMEM_LIMIT),
        name="layer_tail",
    )(x2, attn2, pool2, *mods, g2, w_out_b, w_up_b, w_down_b)


def kernel(x_prompt, x_sample, c_prompt, c_sample, cache_k, cache_v, state_pool, w_ada, b_ada, norm1_g, norm2_g, w_in, q_norm_g, k_norm_g, rel_bias, w_pool, pool_scale, w_out, w_up, w_down):
    b, s_len, d = x_prompt.shape
    db, t, _ = x_sample.shape
    depth = w_ada.shape[0]
    assert depth == 1, "single-layer step"
    n_heads = rel_bias.shape[1]
    a = n_heads * HEAD_DIM
    wb = cache_k.shape[2]
    l = 0

    ada = _adaln(jnp.concatenate([c_prompt, c_sample], axis=0), w_ada[l], b_ada[l])
    mods = [ada[:, i * d:(i + 1) * d] for i in range(N_ADA)]
    mp = [m[:b] for m in mods]
    msm = [m[b:] for m in mods]

    w_in_b = w_in[l].astype(BF16)
    w_out_b = w_out[l].astype(BF16)
    w_up_b = w_up[l].astype(BF16)
    w_down_b = w_down[l].astype(BF16)
    w_pool_b = w_pool[l].astype(BF16)
    g1 = norm1_g[l].reshape(1, d)
    g2 = norm2_g[l].reshape(1, d)
    qg = jnp.tile(q_norm_g[l], n_heads).reshape(1, a)
    kg = jnp.tile(k_norm_g[l], n_heads).reshape(1, a)
    ps = pool_scale[l].reshape(1, -1)
    seg_i = jnp.arange(SEG_W) // HEAD_DIM
    seg = jnp.where(seg_i[:, None] == seg_i[None, :], 1.0 / HEAD_DIM, 0.0).astype(BF16)

    xp2 = x_prompt.reshape(b * s_len, d)
    q, k, v, pool, u_last, kt, vt = _mixer_in(xp2, mp[0], mp[1], g1, w_in_b, qg, kg, seg, w_pool_b, ps, None,
                                              tm=1024, rows_per_mod=s_len, pos0=0, feature_major=True)
    attn = _prompt_attention(q.reshape(b, s_len, a), k.reshape(b, s_len, a), v.reshape(b, s_len, a), rel_bias)
    keep = min(max(w for w, _ in DILATED_BRANCHES), s_len)
    win = lambda c: jnp.transpose(c.reshape(b, n_heads, HEAD_DIM, s_len), (0, 3, 1, 2))[None, :, s_len - keep:]
    k_win_prompt = win(kt)
    v_win_prompt = win(vt)
    pool_prompt = u_last[None, :, POOL_PAD - POOL_CTX:]

    xs2 = x_sample.reshape(db * t, d)
    qs, ks, vs, pool_s, us_last = _mixer_in(xs2, msm[0], msm[1], g1, w_in_b, qg, kg, seg, w_pool_b, ps,
                                            state_pool[l], tm=256, rows_per_mod=t, pos0=PAST_LEN)
    to_fm = lambda c: jnp.transpose(c, (0, 2, 3, 1)).reshape(db, a, wb)
    from_fm = lambda c: jnp.transpose(c.reshape(db, n_heads, HEAD_DIM, wb), (0, 3, 1, 2))[None]
    y_prompt, attn_s, k_win, v_win = _tail_and_sample_attention(
        xp2, attn.reshape(b * s_len, a), pool,
        mp[2], mp[3], mp[4], mp[5], g2, w_out_b, w_up_b, w_down_b,
        qs.reshape(db, t, a), ks.reshape(db, t, a), vs.reshape(db, t, a),
        to_fm(cache_k[l]), to_fm(cache_v[l]), rel_bias, tm=512, rows_per_mod=s_len)
    y_prompt = y_prompt.reshape(b, s_len, d)
    y_sample = _layer_tail(xs2, attn_s.reshape(db * t, a), pool_s,
                           msm[2], msm[3], msm[4], msm[5], g2, w_out_b, w_up_b, w_down_b,
                           tm=db * t, rows_per_mod=t).reshape(db, t, d)
    k_win_sample = from_fm(k_win)
    v_win_sample = from_fm(v_win)
    pool_sample = us_last[None, :, POOL_PAD - POOL_CTX:]

    return (y_prompt, y_sample, k_win_prompt, v_win_prompt, pool_prompt,
            k_win_sample, v_win_sample, pool_sample)
```

```python
import functools
import math

import jax
import jax.numpy as jnp
from jax import lax
from jax.experimental import pallas as pl
from jax.experimental.pallas import tpu as pltpu

F32 = jnp.float32
BF16 = jnp.bfloat16

HEAD_DIM = 64
DILATED_BRANCHES = ((128, 1), (512, 4), (2048, 16))
NUM_BUCKETS = 32
MAX_DISTANCE = 2048
POOL_WINDOWS = (2, 4, 8, 16)
PAST_LEN = 8192
POOL_CTX = max(POOL_WINDOWS) - 1
POOL_PAD = 16
POOL_LEAD = max(POOL_WINDOWS) // 2
N_ADA = 6
EPS = 1e-6
NEG_INF = -1e30
LOG2E = math.log2(math.e)

LANES = 128
SEG_W = 256
VMEM_LIMIT = 56 * 1024 * 1024
VMEM_LIMIT_BIG = 62 * 1024 * 1024
RING_DEPTH = 3


def _rel_bucket(dist):
    exact = NUM_BUCKETS // 2
    d = jnp.maximum(dist.astype(F32), 1.0)
    large = exact + (jnp.log(d / exact) / math.log(MAX_DISTANCE / exact)
                     * (NUM_BUCKETS - exact)).astype(jnp.int32)
    large = jnp.minimum(large, NUM_BUCKETS - 1)
    return jnp.where(dist < exact, dist, large)


def _bias_from_buckets(bkt, relb_ref, head):
    out = jnp.full(bkt.shape, NEG_INF, F32)
    for b in range(NUM_BUCKETS):
        out = jnp.where(bkt == b, relb_ref[b, head], out)
    return out


def _adaln_kernel(c_ref, w_ref, b_ref, o_ref):
    c = c_ref[...]
    s = c / (1.0 + jnp.exp(-c))
    o_ref[...] = jnp.dot(s.astype(BF16), w_ref[...].astype(BF16), preferred_element_type=F32) + b_ref[...]


def _adaln(c, w, b, tn=1024):
    m, d = c.shape
    n = w.shape[1]
    return pl.pallas_call(
        _adaln_kernel,
        out_shape=jax.ShapeDtypeStruct((m, n), F32),
        grid=(n // tn,),
        in_specs=[pl.BlockSpec((m, d), lambda j: (0, 0)),
                  pl.BlockSpec((d, tn), lambda j: (0, j)),
                  pl.BlockSpec((1, tn), lambda j: (0, j))],
        out_specs=pl.BlockSpec((m, tn), lambda j: (0, j)),
        compiler_params=pltpu.CompilerParams(dimension_semantics=("arbitrary",),
                                             vmem_limit_bytes=VMEM_LIMIT),
        name="adaln",
    )(c, w, b.reshape(1, n))


def _modulated_norm(x, g, shift, scale, groups):
    tm, d = x.shape
    ms = jnp.mean(x * x, axis=-1, keepdims=True)
    y = x * lax.rsqrt(ms + EPS) * g
    if groups == 1:
        return y * (1.0 + scale) + shift
    y3 = y.reshape(groups, tm // groups, d)
    return (y3 * (1.0 + scale[:, None, :]) + shift[:, None, :]).reshape(tm, d)


def _per_group(v, x, groups):
    tm, d = x.shape
    if groups == 1:
        return v * x
    return (v[:, None, :] * x.reshape(groups, tm // groups, d)).reshape(tm, d)


def _pool_mix(u3, ctx, wp_ref, ps_ref, buf_a, buf_b, pos):
    nb, t, p = u3.shape
    grp = p // len(POOL_WINDOWS)
    n = POOL_PAD + t
    ctx0 = POOL_LEAD + POOL_PAD - POOL_CTX
    buf_a[:, 0:ctx0, :] = jnp.zeros((nb, ctx0, p), F32)
    buf_b[:, 0:POOL_LEAD, :] = jnp.zeros((nb, POOL_LEAD, p), F32)
    buf_a[:, ctx0:ctx0 + POOL_CTX, :] = ctx
    buf_a[:, ctx0 + POOL_CTX:, :] = u3
    last = buf_a[:, POOL_LEAD + n - POOL_PAD:POOL_LEAD + n, :]

    src, dst = buf_a, buf_b
    where = []
    for g, w in enumerate(POOL_WINDOWS):
        assert w == 2 ** (g + 1)
        cols = slice(g * grp, p)
        dst[:, POOL_LEAD:POOL_LEAD + n, cols] = (src[:, POOL_LEAD:POOL_LEAD + n, cols]
                                                  + src[:, POOL_LEAD - w // 2:POOL_LEAD - w // 2 + n, cols])
        where.append(dst)
        src, dst = dst, src

    new0 = POOL_LEAD + POOL_PAD
    outs = []
    for g, w in enumerate(POOL_WINDOWS):
        cols = slice(g * grp, (g + 1) * grp)
        win = where[g][:, new0:new0 + t, cols]
        cnt = jnp.minimum(float(w), pos + 1.0)
        pooled = (win / cnt - u3[:, :, cols]).reshape(nb * t, grp)
        y = jnp.dot(pooled.astype(BF16), wp_ref[g], preferred_element_type=F32)
        outs.append((y * ps_ref[:, cols]).reshape(nb, t, grp))
    return jnp.concatenate(outs, axis=-1), last


def _mixer_in_kernel(x_ref, shift_ref, scale_ref, g1_ref, w_ref, qg_ref, kg_ref, seg_ref, wp_ref, ps_ref,
                     *refs, groups, tiles_per_seq, pos0, feature_major):
    refs = list(refs)
    ctx_ref = None if tiles_per_seq else refs.pop(0)
    q_ref, k_ref, v_ref, pool_ref, last_ref = refs[:5]
    refs = refs[5:]
    if feature_major:
        kt_ref, vt_ref = refs[:2]
        refs = refs[2:]
    buf_a, buf_b = refs[:2]
    tm, a = q_ref.shape
    p = pool_ref.shape[1]
    t = tm // groups
    row = lax.broadcasted_iota(jnp.int32, (1, t, 1), 1)
    if tiles_per_seq:
        carry = refs[2]
        tile = pl.program_id(0) % tiles_per_seq

        @pl.when(tile == 0)
        def _():
            carry[...] = jnp.zeros(carry.shape, F32)

        ctx = carry[:, POOL_PAD - POOL_CTX:, :]
        pos = (pos0 + tile * tm + row).astype(F32)
    else:
        ctx = ctx_ref[...]
        pos = (pos0 + row).astype(F32)

    h = _modulated_norm(x_ref[...], g1_ref[...], shift_ref[...], scale_ref[...], groups)
    hb = h.astype(BF16)

    def head_norm(z, g):
        parts = []
        for c in range(a // SEG_W):
            zc = z[:, c * SEG_W:(c + 1) * SEG_W]
            ms = jnp.dot((zc * zc).astype(BF16), seg_ref[...], preferred_element_type=F32)
            parts.append(zc * lax.rsqrt(ms + EPS))
        return jnp.concatenate(parts, axis=-1) * g

    zq = jnp.dot(hb, w_ref[:, 0:a], preferred_element_type=F32)
    q_ref[...] = head_norm(zq, qg_ref[...]) * (HEAD_DIM ** -0.5)
    zk = jnp.dot(hb, w_ref[:, a:2 * a], preferred_element_type=F32)
    k = head_norm(zk, kg_ref[...])
    k_ref[...] = k
    v = jnp.dot(hb, w_ref[:, 2 * a:3 * a], preferred_element_type=F32)
    v_ref[...] = v
    if feature_major:
        kt_ref[...] = k.T
        vt_ref[...] = v.T

    u = jnp.dot(hb, w_ref[:, 3 * a:], preferred_element_type=F32)
    pool, last = _pool_mix(u.reshape(groups, t, p), ctx, wp_ref, ps_ref, buf_a, buf_b, pos)
    pool_ref[...] = pool.reshape(tm, p).astype(pool_ref.dtype)
    last_ref[...] = last
    if tiles_per_seq:
        carry[...] = last


def _mixer_in(x2, shift, scale, g1, w_in_b, qg, kg, seg, w_pool_b, pool_scale, ctx, *, tm, rows_per_mod, pos0,
              feature_major=False):
    rows, d = x2.shape
    d_in = w_in_b.shape[1]
    a = qg.shape[1]
    p = d_in - 3 * a
    n_seq = rows // rows_per_mod
    g = len(POOL_WINDOWS)
    const = lambda i: (0, 0)
    row_spec = lambda w: pl.BlockSpec((tm, w), lambda i: (i, 0))
    in_specs = [row_spec(d), None, None,
                pl.BlockSpec((1, d), const),
                pl.BlockSpec((d, d_in), const),
                pl.BlockSpec((1, a), const), pl.BlockSpec((1, a), const),
                pl.BlockSpec((SEG_W, SEG_W), const),
                pl.BlockSpec((g, p // g, p // g), lambda i: (0, 0, 0)),
                pl.BlockSpec((1, p), const)]
    args = [x2, shift, scale, g1, w_in_b, qg, kg, seg, w_pool_b, pool_scale]
    if rows_per_mod >= tm:
        assert ctx is None and rows_per_mod % tm == 0
        groups = 1
        per = rows_per_mod // tm
        args[1] = shift.reshape(-1, 1, d)
        args[2] = scale.reshape(-1, 1, d)
        in_specs[1] = in_specs[2] = pl.BlockSpec((None, 1, d), lambda i: (i // per, 0, 0))
        last_spec = pl.BlockSpec((1, POOL_PAD, p), lambda i: (i // per, 0, 0))
        scratch = [pltpu.VMEM((1, POOL_PAD, p), F32)]
    else:
        assert ctx is not None and not feature_major and tm % rows_per_mod == 0
        groups = tm // rows_per_mod
        per = None
        in_specs[1] = in_specs[2] = pl.BlockSpec((groups, d), lambda i: (i, 0))
        in_specs.append(pl.BlockSpec((groups, POOL_CTX, p), lambda i: (i, 0, 0)))
        args.append(ctx)
        last_spec = pl.BlockSpec((groups, POOL_PAD, p), lambda i: (i, 0, 0))
        scratch = []
    out_shape = ([jax.ShapeDtypeStruct((rows, a), F32)] * 3
                 + [jax.ShapeDtypeStruct((rows, p), BF16), jax.ShapeDtypeStruct((n_seq, POOL_PAD, p), F32)])
    out_specs = [row_spec(a)] * 3 + [row_spec(p), last_spec]
    if feature_major:
        out_shape += [jax.ShapeDtypeStruct((n_seq, a, rows_per_mod), F32)] * 2
        out_specs += [pl.BlockSpec((None, a, tm), lambda i: (i // per, 0, i % per))] * 2
    pool_buf = pltpu.VMEM((groups, POOL_LEAD + POOL_PAD + tm // groups, p), F32)
    return pl.pallas_call(
        functools.partial(_mixer_in_kernel, groups=groups, tiles_per_seq=per, pos0=pos0,
                          feature_major=feature_major),
        out_shape=out_shape,
        grid=(rows // tm,),
        in_specs=in_specs,
        out_specs=out_specs,
        scratch_shapes=[pool_buf, pool_buf] + scratch,
        compiler_params=pltpu.CompilerParams(dimension_semantics=("arbitrary",),
                                             vmem_limit_bytes=VMEM_LIMIT),
        name="mixer_in",
    )(*args)


def _prompt_attn_kernel(relb_ref, bkt_ref, q_ref, k_ref, v_ref, o_ref,
                        qd, kd, vd, bias_s, acc_s, m_s, l_s, mid_s, *, blk, group):
    s_len = q_ref.shape[0]
    nbr = len(DILATED_BRANCHES)
    mid = DILATED_BRANCHES[1][1]
    msub = s_len // mid
    pair = pl.program_id(0)
    head0 = lax.broadcasted_iota(jnp.int32, (1, LANES), 1) < HEAD_DIM

    @pl.when(pl.program_id(1) == 0)
    def _():
        for bi in range(nbr):
            for hh in range(2):
                bias_s[bi, hh * blk:(hh + 1) * blk, :] = LOG2E * _bias_from_buckets(bkt_ref[bi], relb_ref,
                                                                                      2 * pair + hh)

    for src, dst, scale in ((q_ref, qd, LOG2E), (k_ref, kd, None), (v_ref, vd, None)):
        for bi, (_, dil) in enumerate(DILATED_BRANCHES):
            sub = s_len // dil
            for r in range(dil):
                if dil == 1:
                    rows = src[...]
                elif dil <= mid:
                    rows = src[pl.ds(r, sub, stride=dil), :]
                    if dil == mid:
                        mid_s[r * sub:(r + 1) * sub, :] = rows
                else:
                    rows = mid_s[pl.ds((r % mid) * msub + r // mid, sub, stride=dil // mid), :]
                if scale is not None:
                    rows = rows * scale
                dst[bi, r * sub:(r + 1) * sub, :] = rows.astype(BF16)

    def unit(bi, base, dst, first):
        nk = blk if first else 2 * blk
        kbase = base if first else base - blk
        qb = qd[bi, pl.ds(base, blk), :]
        zero = jnp.zeros_like(qb)
        q2 = jnp.concatenate([jnp.where(head0, qb, zero), jnp.where(head0, zero, qb)], axis=0)
        s = lax.dot_general(q2, kd[bi, pl.ds(kbase, nk), :], (((1,), (1,)), ((), ())),
                            preferred_element_type=F32)
        s = s + (bias_s[bi, :, blk:] if first else bias_s[bi])
        m = jnp.max(s, axis=-1, keepdims=True)
        p = jnp.exp2(s - m)
        l = jnp.sum(p, axis=-1, keepdims=True)
        acc = jnp.dot(p.astype(BF16), vd[bi, pl.ds(kbase, nk), :], preferred_element_type=F32)
        full = (blk, LANES)
        acc_s[bi, dst, :] = jnp.where(head0, acc[:blk], acc[blk:])
        m_s[bi, dst, :] = jnp.where(head0, jnp.broadcast_to(m[:blk], full), jnp.broadcast_to(m[blk:], full))
        l_s[bi, dst, :] = jnp.where(head0, jnp.broadcast_to(l[:blk], full), jnp.broadcast_to(l[blk:], full))

    for bi, (_, dil) in enumerate(DILATED_BRANCHES):
        sub = s_len // dil
        nb = sub // blk
        if dil <= mid:
            def residue(r, carry, bi=bi, sub=sub, nb=nb):
                base = pl.multiple_of(r * sub, blk)
                unit(bi, base, pl.ds(base, blk), True)

                def later(n, c):
                    b2 = pl.multiple_of(r * sub + n * blk, blk)
                    unit(bi, b2, pl.ds(b2, blk), False)
                    return c

                lax.fori_loop(1, nb, later, 0, unroll=min(group, nb - 1))
                return carry

            lax.fori_loop(0, dil, residue, 0, unroll=max(1, min(dil, group // nb)))
        else:
            step = dil // mid
            for r in range(dil):
                for n in range(nb):
                    start = (r % mid) * msub + r // mid + n * blk * step
                    unit(bi, r * sub + n * blk, pl.ds(start, blk, stride=step), n == 0)

    for r in range(mid):
        for c in range(msub // blk):
            res = slice(r * msub + c * blk, r * msub + (c + 1) * blk)
            nat = pl.ds(c * blk * mid + r, blk, stride=mid)
            rows = [nat] + [res] * (nbr - 1)
            m = [m_s[bi, rows[bi], :] for bi in range(nbr)]
            top = functools.reduce(jnp.maximum, m)
            num = jnp.zeros((blk, LANES), F32)
            den = jnp.zeros((blk, LANES), F32)
            for bi in range(nbr):
                w = jnp.exp2(m[bi] - top)
                num = num + w * acc_s[bi, rows[bi], :]
                den = den + w * l_s[bi, rows[bi], :]
            o_ref[nat, :] = num / den


def _prompt_attention(q, k, v, rel_bias, blk=128, group=16):
    b, s_len, a = q.shape
    n_pairs = a // LANES
    nbr = len(DILATED_BRANCHES)
    assert all(w // d == blk and (s_len // d) % blk == 0 for w, d in DILATED_BRANCHES)
    qi = jnp.arange(blk)[:, None]
    ki = jnp.arange(2 * blk)[None, :]
    dist = qi + blk - ki
    ok = (dist >= 0) & (dist <= blk)
    bkt = jnp.stack([jnp.where(ok, _rel_bucket(jnp.clip(dist, 0, blk) * d), -1)
                     for _, d in DILATED_BRANCHES]).astype(jnp.int32)
    seq_spec = pl.BlockSpec((None, s_len, LANES), lambda p, i: (i, 0, p))
    return pl.pallas_call(
        functools.partial(_prompt_attn_kernel, blk=blk, group=group),
        out_shape=jax.ShapeDtypeStruct((b, s_len, a), F32),
        grid=(n_pairs, b),
        in_specs=[pl.BlockSpec(memory_space=pltpu.SMEM),
                  pl.BlockSpec((nbr, blk, 2 * blk), lambda p, i: (0, 0, 0)),
                  seq_spec, seq_spec, seq_spec],
        out_specs=seq_spec,
        scratch_shapes=[pltpu.VMEM((nbr, s_len, LANES), BF16)] * 3
                       + [pltpu.VMEM((nbr, 2 * blk, 2 * blk), F32)]
                       + [pltpu.VMEM((nbr, s_len, LANES), F32)] * 3
                       + [pltpu.VMEM((s_len, LANES), F32)],
        compiler_params=pltpu.CompilerParams(dimension_semantics=("arbitrary", "arbitrary"),
                                             vmem_limit_bytes=VMEM_LIMIT),
        name="prompt_attn",
    )(rel_bias, bkt, q, k, v)


def _sample_tables(t, wb, nk):
    dist = wb + jnp.arange(t)[:, None] - jnp.arange(nk)[None, :]
    mult = jnp.zeros((t, nk), jnp.int32)
    for w, d in DILATED_BRANCHES:
        mult = mult + ((dist >= 0) & (dist <= w) & (dist % d == 0)).astype(jnp.int32)
    bkt = jnp.where(mult > 0, _rel_bucket(jnp.maximum(dist, 0)), -1).astype(jnp.int32)
    return bkt, mult.astype(F32)


def _sample_slab(bias, mult, q_ref, kn_ref, vn_ref, ck_ref, cv_ref, o_ref, ko_ref, vo_ref, kall, vall,
                 row_chunk):
    t, slab = q_ref.shape
    wb = ck_ref.shape[1]
    nk = kall.shape[1]
    pad = nk - wb
    hps = slab // HEAD_DIM

    def append(new_ref, cache_ref, out_ref, all_ref):
        new_t = jnp.concatenate([new_ref[...], jnp.zeros((pad - t, slab), F32)], axis=0).T
        for c in range(slab // row_chunk):
            rows = slice(c * row_chunk, (c + 1) * row_chunk)
            ext = jnp.concatenate([cache_ref[rows, :], new_t[rows, :]], axis=1)
            all_ref[rows, :] = ext.astype(BF16)
            out_ref[rows, :] = pltpu.roll(ext, nk - t, axis=1)[:, 0:wb]

    append(kn_ref, ck_ref, ko_ref, kall)
    append(vn_ref, cv_ref, vo_ref, vall)

    lane = lax.broadcasted_iota(jnp.int32, (hps * t, slab), 1)
    row = lax.broadcasted_iota(jnp.int32, (hps * t, slab), 0)
    own = (lane // HEAD_DIM) == (row // t)
    qs = jnp.where(own, jnp.concatenate([q_ref[...]] * hps, axis=0), 0.0).astype(BF16)
    s = jnp.dot(qs, kall[...], preferred_element_type=F32) + bias
    m = jnp.max(s, axis=-1, keepdims=True)
    p = jnp.exp(s - m) * jnp.concatenate([mult] * hps, axis=0)
    l = jnp.sum(p, axis=-1, keepdims=True)
    o = lax.dot_general(p.astype(BF16), vall[...], (((1,), (1,)), ((), ())),
                        preferred_element_type=F32) / l
    o = jnp.where(own, o, 0.0)
    res = o[0:t]
    for hh in range(1, hps):
        res = res + o[hh * t:(hh + 1) * t]
    o_ref[...] = res


def _tail_sample_kernel(x_ref, attn_ref, pool_ref, gate1_ref, shift2_ref, scale2_ref, gate2_ref, g2_ref,
                        wout_ref, wup_ref, wdown_ref,
                        relb_ref, bkt_ref, mult_ref, q_ref, kn_ref, vn_ref, ck_ref, cv_ref,
                        y_ref, o_ref, ko_ref, vo_ref,
                        hb_s, acc_s, kall, vall, bias_s, ring_k, ring_v, ring_sem,
                        *, n_slab, slab, row_chunk, ff_chunk):
    i = pl.program_id(0)
    j = pl.program_id(1)
    n_phase = pl.num_programs(1)
    a = attn_ref.shape[1]
    t = q_ref.shape[1]
    hps = slab // HEAD_DIM
    step = i * n_phase + j
    sq = j // n_slab
    cols = pl.ds(pl.multiple_of((j % n_slab) * slab, slab), slab)
    q_v, kn_v, vn_v, o_v = (r.at[sq, :, cols] for r in (q_ref, kn_ref, vn_ref, o_ref))
    n_steps = pl.num_programs(0) * n_phase

    def window_copies(unit, slot):
        seq = unit // n_slab
        rows = pl.ds(pl.multiple_of((unit % n_slab) * slab, slab), slab)
        return [pltpu.make_async_copy(src.at[seq, rows, :], ring.at[slot], ring_sem.at[slot, which])
                for which, (src, ring) in enumerate(((ck_ref, ring_k), (cv_ref, ring_v)))]

    @pl.when(step == 0)
    def _():
        for u in range(RING_DEPTH - 1):
            for c in window_copies(u, u):
                c.start()

    ahead = step + (RING_DEPTH - 1)

    @pl.when(ahead < n_steps)
    def _():
        for c in window_copies(ahead, ahead % RING_DEPTH):
            c.start()

    @pl.when((i == 0) & (j == 0))
    def _():
        for h in range(n_slab * hps):
            bias_s[h // hps, (h % hps) * t:(h % hps + 1) * t, :] = _bias_from_buckets(bkt_ref[...], relb_ref, h)

    def phase(first):
        slot = step % RING_DEPTH
        for c in window_copies(step, slot):
            c.wait()
        _sample_slab(bias_s[step % n_slab], mult_ref[...], q_v, kn_v, vn_v, ring_k.at[slot],
                     ring_v.at[slot], o_v, ko_ref, vo_ref, kall, vall, row_chunk)
        if first:
            mix = jnp.dot(attn_ref[...].astype(BF16), wout_ref[0:a, :], preferred_element_type=F32)
            mix = mix + jnp.dot(pool_ref[...].astype(BF16), wout_ref[a:, :], preferred_element_type=F32)
            x1 = x_ref[...] + gate1_ref[...] * mix
            y_ref[...] = x1
            hb = _modulated_norm(x1, g2_ref[...], shift2_ref[...], scale2_ref[...], 1).astype(BF16)
            hb_s[...] = hb
            ff = pl.ds(0, ff_chunk)
        else:
            hb = hb_s[...]
            ff = pl.ds(pl.multiple_of(j * ff_chunk, ff_chunk), ff_chunk)
        act = jnp.maximum(jnp.dot(hb, wup_ref[:, ff], preferred_element_type=F32), 0.0)
        part = jnp.dot((act * act).astype(BF16), wdown_ref[ff, :], preferred_element_type=F32)
        if first:
            acc_s[...] = part
        else:
            acc_s[...] += part

    pl.when(j == 0)(functools.partial(phase, True))
    pl.when(j != 0)(functools.partial(phase, False))

    @pl.when(j == n_phase - 1)
    def _():
        y_ref[...] = y_ref[...] + gate2_ref[...] * acc_s[...]


def _tail_and_sample_attention(x2, attn2, pool2, gate1, shift2, scale2, gate2, g2, w_out_b, w_up_b, w_down_b,
                               q, k_new, v_new, cache_kt, cache_vt, rel_bias,
                               *, tm, rows_per_mod, ff_chunk=1024, slab=256, pad=128, row_chunk=64):
    rows, d = x2.shape
    a = attn2.shape[1]
    p = pool2.shape[1]
    d_ff = w_up_b.shape[1]
    db, t, _ = q.shape
    wb = cache_kt.shape[2]
    nk = wb + pad
    n_slab = a // slab
    n_phase = d_ff // ff_chunk
    n_tiles = rows // tm
    hps = slab // HEAD_DIM
    assert pad >= t and wb >= max(w for w, _ in DILATED_BRANCHES)
    assert n_tiles * n_phase == db * n_slab and rows_per_mod % tm == 0
    per = rows_per_mod // tm
    bkt, mult = _sample_tables(t, wb, nk)
    mods = [m.reshape(-1, 1, d) for m in (gate1, shift2, scale2, gate2)]

    row_spec = lambda w: pl.BlockSpec((tm, w), lambda i, j: (i, 0))
    mod_spec = pl.BlockSpec((None, 1, d), lambda i, j: (i // per, 0, 0))
    resident = lambda shape: pl.BlockSpec(shape, lambda i, j: (0,) * len(shape), pipeline_mode=pl.Buffered(1))
    seq = lambda i, j: (i * n_phase + j) // n_slab
    sl = lambda i, j: (i * n_phase + j) % n_slab
    assert n_phase % n_slab == 0
    new_spec = pl.BlockSpec((n_phase // n_slab, t, a), lambda i, j: (i, 0, 0))
    win_spec = pl.BlockSpec((None, slab, wb), lambda i, j: (seq(i, j), sl(i, j), 0))
    tab_spec = pl.BlockSpec((t, nk), lambda i, j: (0, 0))
    return pl.pallas_call(
        functools.partial(_tail_sample_kernel, n_slab=n_slab, slab=slab, row_chunk=row_chunk,
                          ff_chunk=ff_chunk),
        out_shape=[jax.ShapeDtypeStruct((rows, d), F32),
                   jax.ShapeDtypeStruct((db, t, a), F32),
                   jax.ShapeDtypeStruct((db, a, wb), F32),
                   jax.ShapeDtypeStruct((db, a, wb), F32)],
        grid=(n_tiles, n_phase),
        in_specs=[row_spec(d), row_spec(a), row_spec(p), mod_spec, mod_spec, mod_spec, mod_spec,
                  pl.BlockSpec((1, d), lambda i, j: (0, 0)),
                  resident((a + p, d)), resident((d, d_ff)), resident((d_ff, d)),
                  pl.BlockSpec(memory_space=pltpu.SMEM), tab_spec, tab_spec,
                  new_spec, new_spec, new_spec,
                  pl.BlockSpec(memory_space=pl.ANY), pl.BlockSpec(memory_space=pl.ANY)],
        out_specs=[row_spec(d), new_spec, win_spec, win_spec],
        scratch_shapes=[pltpu.VMEM((tm, d), BF16), pltpu.VMEM((tm, d), F32),
                        pltpu.VMEM((slab, nk), BF16), pltpu.VMEM((slab, nk), BF16),
                        pltpu.VMEM((n_slab, hps * t, nk), F32),
                        pltpu.VMEM((RING_DEPTH, slab, wb), F32), pltpu.VMEM((RING_DEPTH, slab, wb), F32),
                        pltpu.SemaphoreType.DMA((RING_DEPTH, 2))],
        compiler_params=pltpu.CompilerParams(dimension_semantics=("arbitrary", "arbitrary"),
                                             vmem_limit_bytes=VMEM_LIMIT_BIG),
        name="tail_sample",
    )(x2, attn2, pool2, *mods, g2, w_out_b, w_up_b, w_down_b,
      rel_bias, bkt, mult, q, k_new, v_new, cache_kt, cache_vt)


def _tail_kernel(x_ref, attn_ref, pool_ref, gate1_ref, shift2_ref, scale2_ref, gate2_ref, g2_ref,
                 wout_ref, wup_ref, wdown_ref, y_ref, *, groups, ff_chunk):
    a = attn_ref.shape[1]
    d_ff = wup_ref.shape[1]
    mix = jnp.dot(attn_ref[...].astype(BF16), wout_ref[0:a, :], preferred_element_type=F32)
    mix = mix + jnp.dot(pool_ref[...].astype(BF16), wout_ref[a:, :], preferred_element_type=F32)
    x1 = x_ref[...] + _per_group(gate1_ref[...], mix, groups)
    hb = _modulated_norm(x1, g2_ref[...], shift2_ref[...], scale2_ref[...], groups).astype(BF16)
    f = jnp.zeros(x1.shape, F32)
    for c in range(d_ff // ff_chunk):
        cols = slice(c * ff_chunk, (c + 1) * ff_chunk)
        act = jnp.maximum(jnp.dot(hb, wup_ref[:, cols], preferred_element_type=F32), 0.0)
        f = f + jnp.dot((act * act).astype(BF16), wdown_ref[cols, :], preferred_element_type=F32)
    y_ref[...] = x1 + _per_group(gate2_ref[...], f, groups)


def _layer_tail(x2, attn2, pool2, gate1, shift2, scale2, gate2, g2, w_out_b, w_up_b, w_down_b,
                *, tm, rows_per_mod, ff_chunk=1024):
    rows, d = x2.shape
    a = attn2.shape[1]
    p = pool2.shape[1]
    d_ff = w_up_b.shape[1]
    mods = [gate1, shift2, scale2, gate2]
    if rows_per_mod >= tm:
        groups = 1
        per = rows_per_mod // tm
        mods = [m.reshape(-1, 1, d) for m in mods]
        mod_spec = pl.BlockSpec((None, 1, d), lambda i: (i // per, 0, 0))
    else:
        groups = tm // rows_per_mod
        mod_spec = pl.BlockSpec((groups, d), lambda i: (i, 0))
    const = lambda i: (0, 0)
    row_spec = lambda w: pl.BlockSpec((tm, w), lambda i: (i, 0))
    resident = lambda shape: pl.BlockSpec(shape, const, pipeline_mode=pl.Buffered(1))
    return pl.pallas_call(
        functools.partial(_tail_kernel, groups=groups, ff_chunk=ff_chunk),
        out_shape=jax.ShapeDtypeStruct((rows, d), F32),
        grid=(rows // tm,),
        in_specs=[row_spec(d), row_spec(a), row_spec(p), mod_spec, mod_spec, mod_spec, mod_spec,
                  pl.BlockSpec((1, d), const),
                  resident((a + p, d)), resident((d, d_ff)), resident((d_ff, d))],
        out_specs=row_spec(d),
        compiler_params=pltpu.CompilerParams(dimension_semantics=("arbitrary",),
                                             vmem_limit_bytes=VMEM_LIMIT),
        name="layer_tail",
    )(x2, attn2, pool2, *mods, g2, w_out_b, w_up_b, w_down_b)


def kernel(x_prompt, x_sample, c_prompt, c_sample, cache_k, cache_v, state_pool, w_ada, b_ada, norm1_g, norm2_g, w_in, q_norm_g, k_norm_g, rel_bias, w_pool, pool_scale, w_out, w_up, w_down):
    b, s_len, d = x_prompt.shape
    db, t, _ = x_sample.shape
    depth = w_ada.shape[0]
    assert depth == 1, "single-layer step"
    n_heads = rel_bias.shape[1]
    a = n_heads * HEAD_DIM
    wb = cache_k.shape[2]
    l = 0

    ada = _adaln(jnp.concatenate([c_prompt, c_sample], axis=0), w_ada[l], b_ada[l])
    mods = [ada[:, i * d:(i + 1) * d] for i in range(N_ADA)]
    mp = [m[:b] for m in mods]
    msm = [m[b:] for m in mods]

    w_in_b = w_in[l].astype(BF16)
    w_out_b = w_out[l].astype(BF16)
    w_up_b = w_up[l].astype(BF16)
    w_down_b = w_down[l].astype(BF16)
    w_pool_b = w_pool[l].astype(BF16)
    g1 = norm1_g[l].reshape(1, d)
    g2 = norm2_g[l].reshape(1, d)
    qg = jnp.tile(q_norm_g[l], n_heads).reshape(1, a)
    kg = jnp.tile(k_norm_g[l], n_heads).reshape(1, a)
    ps = pool_scale[l].reshape(1, -1)
    seg_i = jnp.arange(SEG_W) // HEAD_DIM
    seg = jnp.where(seg_i[:, None] == seg_i[None, :], 1.0 / HEAD_DIM, 0.0).astype(BF16)

    xp2 = x_prompt.reshape(b * s_len, d)
    q, k, v, pool, u_last, kt, vt = _mixer_in(xp2, mp[0], mp[1], g1, w_in_b, qg, kg, seg, w_pool_b, ps, None,
                                              tm=1024, rows_per_mod=s_len, pos0=0, feature_major=True)
    attn = _prompt_attention(q.reshape(b, s_len, a), k.reshape(b, s_len, a), v.reshape(b, s_len, a), rel_bias)
    keep = min(max(w for w, _ in DILATED_BRANCHES), s_len)
    win = lambda c: jnp.transpose(c.reshape(b, n_heads, HEAD_DIM, s_len), (0, 3, 1, 2))[None, :, s_len - keep:]
    k_win_prompt = win(kt)
    v_win_prompt = win(vt)
    pool_prompt = u_last[None, :, POOL_PAD - POOL_CTX:]

    xs2 = x_sample.reshape(db * t, d)
    qs, ks, vs, pool_s, us_last = _mixer_in(xs2, msm[0], msm[1], g1, w_in_b, qg, kg, seg, w_pool_b, ps,
                                            state_pool[l], tm=256, rows_per_mod=t, pos0=PAST_LEN)
    to_fm = lambda c: jnp.transpose(c, (0, 2, 3, 1)).reshape(db, a, wb)
    from_fm = lambda c: jnp.transpose(c.reshape(db, n_heads, HEAD_DIM, wb), (0, 3, 1, 2))[None]
    y_prompt, attn_s, k_win, v_win = _tail_and_sample_attention(
        xp2, attn.reshape(b * s_len, a), pool,
        mp[2], mp[3], mp[4], mp[5], g2, w_out_b, w_up_b, w_down_b,
        qs.reshape(db, t, a), ks.reshape(db, t, a), vs.reshape(db, t, a),
        to_fm(cache_k[l]), to_fm(cache_v[l]), rel_bias, tm=512, rows_per_mod=s_len)
    y_prompt = y_prompt.reshape(b, s_len, d)
    y_sample = _layer_tail(xs2, attn_s.reshape(db * t, a), pool_s,
                           msm[2], msm[3], msm[4], msm[5], g2, w_out_b, w_up_b, w_down_b,
                           tm=256, rows_per_mod=t).reshape(db, t, d)
    k_win_sample = from_fm(k_win)
    v_win_sample = from_fm(v_win)
    pool_sample = us_last[None, :, POOL_PAD - POOL_CTX:]

    return (y_prompt, y_sample, k_win_prompt, v_win_prompt, pool_prompt,
            k_win_sample, v_win_sample, pool_sample)
```

```python
import functools
import math

import jax
import jax.numpy as jnp
from jax import lax
from jax.experimental import pallas as pl
from jax.experimental.pallas import tpu as pltpu

F32 = jnp.float32
BF16 = jnp.bfloat16

HEAD_DIM = 64
DILATED_BRANCHES = ((128, 1), (512, 4), (2048, 16))
NUM_BUCKETS = 32
MAX_DISTANCE = 2048
POOL_WINDOWS = (2, 4, 8, 16)
PAST_LEN = 8192
POOL_CTX = max(POOL_WINDOWS) - 1
POOL_PAD = 16
POOL_LEAD = max(POOL_WINDOWS) // 2
N_ADA = 6
EPS = 1e-6
NEG_INF = -1e30
LOG2E = math.log2(math.e)

LANES = 128
SEG_W = 256
VMEM_LIMIT = 56 * 1024 * 1024
VMEM_LIMIT_BIG = 62 * 1024 * 1024
RING_DEPTH = 4


def _rel_bucket(dist):
    exact = NUM_BUCKETS // 2
    d = jnp.maximum(dist.astype(F32), 1.0)
    large = exact + (jnp.log(d / exact) / math.log(MAX_DISTANCE / exact)
                     * (NUM_BUCKETS - exact)).astype(jnp.int32)
    large = jnp.minimum(large, NUM_BUCKETS - 1)
    return jnp.where(dist < exact, dist, large)


def _bias_from_buckets(bkt, relb_ref, head):
    out = jnp.full(bkt.shape, NEG_INF, F32)
    for b in range(NUM_BUCKETS):
        out = jnp.where(bkt == b, relb_ref[b, head], out)
    return out


def _adaln_kernel(c_ref, w_ref, b_ref, o_ref):
    c = c_ref[...]
    s = c / (1.0 + jnp.exp(-c))
    o_ref[...] = jnp.dot(s.astype(BF16), w_ref[...].astype(BF16), preferred_element_type=F32) + b_ref[...]


def _adaln(c, w, b, tn=1024):
    m, d = c.shape
    n = w.shape[1]
    return pl.pallas_call(
        _adaln_kernel,
        out_shape=jax.ShapeDtypeStruct((m, n), F32),
        grid=(n // tn,),
        in_specs=[pl.BlockSpec((m, d), lambda j: (0, 0)),
                  pl.BlockSpec((d, tn), lambda j: (0, j)),
                  pl.BlockSpec((1, tn), lambda j: (0, j))],
        out_specs=pl.BlockSpec((m, tn), lambda j: (0, j)),
        compiler_params=pltpu.CompilerParams(dimension_semantics=("arbitrary",),
                                             vmem_limit_bytes=VMEM_LIMIT),
        name="adaln",
    )(c, w, b.reshape(1, n))


def _modulated_norm(x, g, shift, scale, groups):
    tm, d = x.shape
    ms = jnp.mean(x * x, axis=-1, keepdims=True)
    y = x * lax.rsqrt(ms + EPS) * g
    if groups == 1:
        return y * (1.0 + scale) + shift
    y3 = y.reshape(groups, tm // groups, d)
    return (y3 * (1.0 + scale[:, None, :]) + shift[:, None, :]).reshape(tm, d)


def _per_group(v, x, groups):
    tm, d = x.shape
    if groups == 1:
        return v * x
    return (v[:, None, :] * x.reshape(groups, tm // groups, d)).reshape(tm, d)


def _pool_mix(u3, ctx, wp_ref, ps_ref, buf_a, buf_b, pos):
    nb, t, p = u3.shape
    grp = p // len(POOL_WINDOWS)
    n = POOL_PAD + t
    ctx0 = POOL_LEAD + POOL_PAD - POOL_CTX
    buf_a[:, 0:ctx0, :] = jnp.zeros((nb, ctx0, p), F32)
    buf_b[:, 0:POOL_LEAD, :] = jnp.zeros((nb, POOL_LEAD, p), F32)
    buf_a[:, ctx0:ctx0 + POOL_CTX, :] = ctx
    buf_a[:, ctx0 + POOL_CTX:, :] = u3
    last = buf_a[:, POOL_LEAD + n - POOL_PAD:POOL_LEAD + n, :]

    src, dst = buf_a, buf_b
    where = []
    for g, w in enumerate(POOL_WINDOWS):
        assert w == 2 ** (g + 1)
        cols = slice(g * grp, p)
        dst[:, POOL_LEAD:POOL_LEAD + n, cols] = (src[:, POOL_LEAD:POOL_LEAD + n, cols]
                                                  + src[:, POOL_LEAD - w // 2:POOL_LEAD - w // 2 + n, cols])
        where.append(dst)
        src, dst = dst, src

    new0 = POOL_LEAD + POOL_PAD
    outs = []
    for g, w in enumerate(POOL_WINDOWS):
        cols = slice(g * grp, (g + 1) * grp)
        win = where[g][:, new0:new0 + t, cols]
        cnt = jnp.minimum(float(w), pos + 1.0)
        pooled = (win / cnt - u3[:, :, cols]).reshape(nb * t, grp)
        y = jnp.dot(pooled.astype(BF16), wp_ref[g], preferred_element_type=F32)
        outs.append((y * ps_ref[:, cols]).reshape(nb, t, grp))
    return jnp.concatenate(outs, axis=-1), last


def _mixer_in_kernel(x_ref, shift_ref, scale_ref, g1_ref, w_ref, qg_ref, kg_ref, seg_ref, wp_ref, ps_ref,
                     *refs, groups, tiles_per_seq, pos0, feature_major):
    refs = list(refs)
    ctx_ref = None if tiles_per_seq else refs.pop(0)
    q_ref, k_ref, v_ref, pool_ref, last_ref = refs[:5]
    refs = refs[5:]
    if feature_major:
        kt_ref, vt_ref = refs[:2]
        refs = refs[2:]
    buf_a, buf_b = refs[:2]
    tm, a = q_ref.shape
    p = pool_ref.shape[1]
    t = tm // groups
    row = lax.broadcasted_iota(jnp.int32, (1, t, 1), 1)
    if tiles_per_seq:
        carry = refs[2]
        tile = pl.program_id(0) % tiles_per_seq

        @pl.when(tile == 0)
        def _():
            carry[...] = jnp.zeros(carry.shape, F32)

        ctx = carry[:, POOL_PAD - POOL_CTX:, :]
        pos = (pos0 + tile * tm + row).astype(F32)
    else:
        ctx = ctx_ref[...]
        pos = (pos0 + row).astype(F32)

    h = _modulated_norm(x_ref[...], g1_ref[...], shift_ref[...], scale_ref[...], groups)
    hb = h.astype(BF16)

    def head_norm(z, g):
        parts = []
        for c in range(a // SEG_W):
            zc = z[:, c * SEG_W:(c + 1) * SEG_W]
            ms = jnp.dot((zc * zc).astype(BF16), seg_ref[...], preferred_element_type=F32)
            parts.append(zc * lax.rsqrt(ms + EPS))
        return jnp.concatenate(parts, axis=-1) * g

    zq = jnp.dot(hb, w_ref[:, 0:a], preferred_element_type=F32)
    q_ref[...] = head_norm(zq, qg_ref[...]) * (HEAD_DIM ** -0.5)
    zk = jnp.dot(hb, w_ref[:, a:2 * a], preferred_element_type=F32)
    k = head_norm(zk, kg_ref[...])
    k_ref[...] = k
    v = jnp.dot(hb, w_ref[:, 2 * a:3 * a], preferred_element_type=F32)
    v_ref[...] = v
    if feature_major:
        kt_ref[...] = k.T
        vt_ref[...] = v.T

    u = jnp.dot(hb, w_ref[:, 3 * a:], preferred_element_type=F32)
    pool, last = _pool_mix(u.reshape(groups, t, p), ctx, wp_ref, ps_ref, buf_a, buf_b, pos)
    pool_ref[...] = pool.reshape(tm, p).astype(pool_ref.dtype)
    last_ref[...] = last
    if tiles_per_seq:
        carry[...] = last


def _mixer_in(x2, shift, scale, g1, w_in_b, qg, kg, seg, w_pool_b, pool_scale, ctx, *, tm, rows_per_mod, pos0,
              feature_major=False):
    rows, d = x2.shape
    d_in = w_in_b.shape[1]
    a = qg.shape[1]
    p = d_in - 3 * a
    n_seq = rows // rows_per_mod
    g = len(POOL_WINDOWS)
    const = lambda i: (0, 0)
    row_spec = lambda w: pl.BlockSpec((tm, w), lambda i: (i, 0))
    in_specs = [row_spec(d), None, None,
                pl.BlockSpec((1, d), const),
                pl.BlockSpec((d, d_in), const),
                pl.BlockSpec((1, a), const), pl.BlockSpec((1, a), const),
                pl.BlockSpec((SEG_W, SEG_W), const),
                pl.BlockSpec((g, p // g, p // g), lambda i: (0, 0, 0)),
                pl.BlockSpec((1, p), const)]
    args = [x2, shift, scale, g1, w_in_b, qg, kg, seg, w_pool_b, pool_scale]
    if rows_per_mod >= tm:
        assert ctx is None and rows_per_mod % tm == 0
        groups = 1
        per = rows_per_mod // tm
        args[1] = shift.reshape(-1, 1, d)
        args[2] = scale.reshape(-1, 1, d)
        in_specs[1] = in_specs[2] = pl.BlockSpec((None, 1, d), lambda i: (i // per, 0, 0))
        last_spec = pl.BlockSpec((1, POOL_PAD, p), lambda i: (i // per, 0, 0))
        scratch = [pltpu.VMEM((1, POOL_PAD, p), F32)]
    else:
        assert ctx is not None and not feature_major and tm % rows_per_mod == 0
        groups = tm // rows_per_mod
        per = None
        in_specs[1] = in_specs[2] = pl.BlockSpec((groups, d), lambda i: (i, 0))
        in_specs.append(pl.BlockSpec((groups, POOL_CTX, p), lambda i: (i, 0, 0)))
        args.append(ctx)
        last_spec = pl.BlockSpec((groups, POOL_PAD, p), lambda i: (i, 0, 0))
        scratch = []
    out_shape = ([jax.ShapeDtypeStruct((rows, a), F32)] * 3
                 + [jax.ShapeDtypeStruct((rows, p), BF16), jax.ShapeDtypeStruct((n_seq, POOL_PAD, p), F32)])
    out_specs = [row_spec(a)] * 3 + [row_spec(p), last_spec]
    if feature_major:
        out_shape += [jax.ShapeDtypeStruct((n_seq, a, rows_per_mod), F32)] * 2
        out_specs += [pl.BlockSpec((None, a, tm), lambda i: (i // per, 0, i % per))] * 2
    pool_buf = pltpu.VMEM((groups, POOL_LEAD + POOL_PAD + tm // groups, p), F32)
    return pl.pallas_call(
        functools.partial(_mixer_in_kernel, groups=groups, tiles_per_seq=per, pos0=pos0,
                          feature_major=feature_major),
        out_shape=out_shape,
        grid=(rows // tm,),
        in_specs=in_specs,
        out_specs=out_specs,
        scratch_shapes=[pool_buf, pool_buf] + scratch,
        compiler_params=pltpu.CompilerParams(dimension_semantics=("arbitrary",),
                                             vmem_limit_bytes=VMEM_LIMIT),
        name="mixer_in",
    )(*args)


def _prompt_attn_kernel(relb_ref, bkt_ref, q_ref, k_ref, v_ref, o_ref,
                        qd, kd, vd, bias_s, acc_s, m_s, l_s, mid_s, *, blk, group):
    s_len = q_ref.shape[0]
    nbr = len(DILATED_BRANCHES)
    mid = DILATED_BRANCHES[1][1]
    msub = s_len // mid
    pair = pl.program_id(0)
    head0 = lax.broadcasted_iota(jnp.int32, (1, LANES), 1) < HEAD_DIM

    @pl.when(pl.program_id(1) == 0)
    def _():
        for bi in range(nbr):
            for hh in range(2):
                bias_s[bi, hh * blk:(hh + 1) * blk, :] = LOG2E * _bias_from_buckets(bkt_ref[bi], relb_ref,
                                                                                      2 * pair + hh)

    for src, dst, scale in ((q_ref, qd, LOG2E), (k_ref, kd, None), (v_ref, vd, None)):
        for bi, (_, dil) in enumerate(DILATED_BRANCHES):
            sub = s_len // dil
            for r in range(dil):
                if dil == 1:
                    rows = src[...]
                elif dil <= mid:
                    rows = src[pl.ds(r, sub, stride=dil), :]
                    if dil == mid:
                        mid_s[r * sub:(r + 1) * sub, :] = rows
                else:
                    rows = mid_s[pl.ds((r % mid) * msub + r // mid, sub, stride=dil // mid), :]
                if scale is not None:
                    rows = rows * scale
                dst[bi, r * sub:(r + 1) * sub, :] = rows.astype(BF16)

    def unit(bi, base, dst, first):
        nk = blk if first else 2 * blk
        kbase = base if first else base - blk
        qb = qd[bi, pl.ds(base, blk), :]
        zero = jnp.zeros_like(qb)
        q2 = jnp.concatenate([jnp.where(head0, qb, zero), jnp.where(head0, zero, qb)], axis=0)
        s = lax.dot_general(q2, kd[bi, pl.ds(kbase, nk), :], (((1,), (1,)), ((), ())),
                            preferred_element_type=F32)
        s = s + (bias_s[bi, :, blk:] if first else bias_s[bi])
        m = jnp.max(s, axis=-1, keepdims=True)
        p = jnp.exp2(s - m)
        l = jnp.sum(p, axis=-1, keepdims=True)
        acc = jnp.dot(p.astype(BF16), vd[bi, pl.ds(kbase, nk), :], preferred_element_type=F32)
        full = (blk, LANES)
        acc_s[bi, dst, :] = jnp.where(head0, acc[:blk], acc[blk:])
        m_s[bi, dst, :] = jnp.where(head0, jnp.broadcast_to(m[:blk], full), jnp.broadcast_to(m[blk:], full))
        l_s[bi, dst, :] = jnp.where(head0, jnp.broadcast_to(l[:blk], full), jnp.broadcast_to(l[blk:], full))

    for bi, (_, dil) in enumerate(DILATED_BRANCHES):
        sub = s_len // dil
        nb = sub // blk
        if dil <= mid:
            def residue(r, carry, bi=bi, sub=sub, nb=nb):
                base = pl.multiple_of(r * sub, blk)
                unit(bi, base, pl.ds(base, blk), True)

                def later(n, c):
                    b2 = pl.multiple_of(r * sub + n * blk, blk)
                    unit(bi, b2, pl.ds(b2, blk), False)
                    return c

                lax.fori_loop(1, nb, later, 0, unroll=min(group, nb - 1))
                return carry

            lax.fori_loop(0, dil, residue, 0, unroll=max(1, min(dil, group // nb)))
        else:
            step = dil // mid
            for r in range(dil):
                for n in range(nb):
                    start = (r % mid) * msub + r // mid + n * blk * step
                    unit(bi, r * sub + n * blk, pl.ds(start, blk, stride=step), n == 0)

    for r in range(mid):
        for c in range(msub // blk):
            res = slice(r * msub + c * blk, r * msub + (c + 1) * blk)
            nat = pl.ds(c * blk * mid + r, blk, stride=mid)
            rows = [nat] + [res] * (nbr - 1)
            m = [m_s[bi, rows[bi], :] for bi in range(nbr)]
            top = functools.reduce(jnp.maximum, m)
            num = jnp.zeros((blk, LANES), F32)
            den = jnp.zeros((blk, LANES), F32)
            for bi in range(nbr):
                w = jnp.exp2(m[bi] - top)
                num = num + w * acc_s[bi, rows[bi], :]
                den = den + w * l_s[bi, rows[bi], :]
            o_ref[nat, :] = num / den


def _prompt_attention(q, k, v, rel_bias, blk=128, group=16):
    b, s_len, a = q.shape
    n_pairs = a // LANES
    nbr = len(DILATED_BRANCHES)
    assert all(w // d == blk and (s_len // d) % blk == 0 for w, d in DILATED_BRANCHES)
    qi = jnp.arange(blk)[:, None]
    ki = jnp.arange(2 * blk)[None, :]
    dist = qi + blk - ki
    ok = (dist >= 0) & (dist <= blk)
    bkt = jnp.stack([jnp.where(ok, _rel_bucket(jnp.clip(dist, 0, blk) * d), -1)
                     for _, d in DILATED_BRANCHES]).astype(jnp.int32)
    seq_spec = pl.BlockSpec((None, s_len, LANES), lambda p, i: (i, 0, p))
    return pl.pallas_call(
        functools.partial(_prompt_attn_kernel, blk=blk, group=group),
        out_shape=jax.ShapeDtypeStruct((b, s_len, a), F32),
        grid=(n_pairs, b),
        in_specs=[pl.BlockSpec(memory_space=pltpu.SMEM),
                  pl.BlockSpec((nbr, blk, 2 * blk), lambda p, i: (0, 0, 0)),
                  seq_spec, seq_spec, seq_spec],
        out_specs=seq_spec,
        scratch_shapes=[pltpu.VMEM((nbr, s_len, LANES), BF16)] * 3
                       + [pltpu.VMEM((nbr, 2 * blk, 2 * blk), F32)]
                       + [pltpu.VMEM((nbr, s_len, LANES), F32)] * 3
                       + [pltpu.VMEM((s_len, LANES), F32)],
        compiler_params=pltpu.CompilerParams(dimension_semantics=("arbitrary", "arbitrary"),
                                             vmem_limit_bytes=VMEM_LIMIT),
        name="prompt_attn",
    )(rel_bias, bkt, q, k, v)


def _sample_tables(t, wb, nk):
    dist = wb + jnp.arange(t)[:, None] - jnp.arange(nk)[None, :]
    mult = jnp.zeros((t, nk), jnp.int32)
    for w, d in DILATED_BRANCHES:
        mult = mult + ((dist >= 0) & (dist <= w) & (dist % d == 0)).astype(jnp.int32)
    bkt = jnp.where(mult > 0, _rel_bucket(jnp.maximum(dist, 0)), -1).astype(jnp.int32)
    return bkt, mult.astype(F32)


def _sample_slab(bias, mult, q_ref, kn_ref, vn_ref, ck_ref, cv_ref, o_ref, ko_ref, vo_ref, kall, vall,
                 row_chunk):
    t, slab = q_ref.shape
    wb = ck_ref.shape[1]
    nk = kall.shape[1]
    pad = nk - wb
    hps = slab // HEAD_DIM

    def append(new_ref, cache_ref, out_ref, all_ref):
        new_t = jnp.concatenate([new_ref[...], jnp.zeros((pad - t, slab), F32)], axis=0).T
        for c in range(slab // row_chunk):
            rows = slice(c * row_chunk, (c + 1) * row_chunk)
            ext = jnp.concatenate([cache_ref[rows, :], new_t[rows, :]], axis=1)
            all_ref[rows, :] = ext.astype(BF16)
            out_ref[rows, :] = pltpu.roll(ext, nk - t, axis=1)[:, 0:wb]

    append(kn_ref, ck_ref, ko_ref, kall)
    append(vn_ref, cv_ref, vo_ref, vall)

    lane = lax.broadcasted_iota(jnp.int32, (hps * t, slab), 1)
    row = lax.broadcasted_iota(jnp.int32, (hps * t, slab), 0)
    own = (lane // HEAD_DIM) == (row // t)
    qs = jnp.where(own, jnp.concatenate([q_ref[...]] * hps, axis=0), 0.0).astype(BF16)
    s = jnp.dot(qs, kall[...], preferred_element_type=F32) + bias
    m = jnp.max(s, axis=-1, keepdims=True)
    p = jnp.exp(s - m) * jnp.concatenate([mult] * hps, axis=0)
    l = jnp.sum(p, axis=-1, keepdims=True)
    o = lax.dot_general(p.astype(BF16), vall[...], (((1,), (1,)), ((), ())),
                        preferred_element_type=F32) / l
    o = jnp.where(own, o, 0.0)
    res = o[0:t]
    for hh in range(1, hps):
        res = res + o[hh * t:(hh + 1) * t]
    o_ref[...] = res


def _tail_sample_kernel(x_ref, attn_ref, pool_ref, gate1_ref, shift2_ref, scale2_ref, gate2_ref, g2_ref,
                        wout_ref, wup_ref, wdown_ref,
                        relb_ref, bkt_ref, mult_ref, q_ref, kn_ref, vn_ref, ck_ref, cv_ref,
                        y_ref, o_ref, ko_ref, vo_ref,
                        hb_s, acc_s, kall, vall, bias_s, ring_k, ring_v, ring_sem,
                        *, n_slab, slab, row_chunk, ff_chunk):
    i = pl.program_id(0)
    j = pl.program_id(1)
    n_phase = pl.num_programs(1)
    a = attn_ref.shape[1]
    t = q_ref.shape[1]
    hps = slab // HEAD_DIM
    step = i * n_phase + j
    sq = j // n_slab
    cols = pl.ds(pl.multiple_of((j % n_slab) * slab, slab), slab)
    q_v, kn_v, vn_v, o_v = (r.at[sq, :, cols] for r in (q_ref, kn_ref, vn_ref, o_ref))
    n_steps = pl.num_programs(0) * n_phase

    def window_copies(unit, slot):
        seq = unit // n_slab
        rows = pl.ds(pl.multiple_of((unit % n_slab) * slab, slab), slab)
        return [pltpu.make_async_copy(src.at[seq, rows, :], ring.at[slot], ring_sem.at[slot, which])
                for which, (src, ring) in enumerate(((ck_ref, ring_k), (cv_ref, ring_v)))]

    @pl.when(step == 0)
    def _():
        for u in range(RING_DEPTH - 1):
            for c in window_copies(u, u):
                c.start()

    ahead = step + (RING_DEPTH - 1)

    @pl.when(ahead < n_steps)
    def _():
        for c in window_copies(ahead, ahead % RING_DEPTH):
            c.start()

    @pl.when((i == 0) & (j == 0))
    def _():
        for h in range(n_slab * hps):
            bias_s[h // hps, (h % hps) * t:(h % hps + 1) * t, :] = _bias_from_buckets(bkt_ref[...], relb_ref, h)

    def phase(first):
        slot = step % RING_DEPTH
        for c in window_copies(step, slot):
            c.wait()
        _sample_slab(bias_s[step % n_slab], mult_ref[...], q_v, kn_v, vn_v, ring_k.at[slot],
                     ring_v.at[slot], o_v, ko_ref, vo_ref, kall, vall, row_chunk)
        if first:
            mix = jnp.dot(attn_ref[...].astype(BF16), wout_ref[0:a, :], preferred_element_type=F32)
            mix = mix + jnp.dot(pool_ref[...].astype(BF16), wout_ref[a:, :], preferred_element_type=F32)
            x1 = x_ref[...] + gate1_ref[...] * mix
            y_ref[...] = x1
            hb = _modulated_norm(x1, g2_ref[...], shift2_ref[...], scale2_ref[...], 1).astype(BF16)
            hb_s[...] = hb
            ff = pl.ds(0, ff_chunk)
        else:
            hb = hb_s[...]
            ff = pl.ds(pl.multiple_of(j * ff_chunk, ff_chunk), ff_chunk)
        act = jnp.maximum(jnp.dot(hb, wup_ref[:, ff], preferred_element_type=F32), 0.0)
        part = jnp.dot((act * act).astype(BF16), wdown_ref[ff, :], preferred_element_type=F32)
        if first:
            acc_s[...] = part
        else:
            acc_s[...] += part

    pl.when(j == 0)(functools.partial(phase, True))
    pl.when(j != 0)(functools.partial(phase, False))

    @pl.when(j == n_phase - 1)
    def _():
        y_ref[...] = y_ref[...] + gate2_ref[...] * acc_s[...]


def _tail_and_sample_attention(x2, attn2, pool2, gate1, shift2, scale2, gate2, g2, w_out_b, w_up_b, w_down_b,
                               q, k_new, v_new, cache_kt, cache_vt, rel_bias,
                               *, tm, rows_per_mod, ff_chunk=1024, slab=256, pad=128, row_chunk=64):
    rows, d = x2.shape
    a = attn2.shape[1]
    p = pool2.shape[1]
    d_ff = w_up_b.shape[1]
    db, t, _ = q.shape
    wb = cache_kt.shape[2]
    nk = wb + pad
    n_slab = a // slab
    n_phase = d_ff // ff_chunk
    n_tiles = rows // tm
    hps = slab // HEAD_DIM
    assert pad >= t and wb >= max(w for w, _ in DILATED_BRANCHES)
    assert n_tiles * n_phase == db * n_slab and rows_per_mod % tm == 0
    per = rows_per_mod // tm
    bkt, mult = _sample_tables(t, wb, nk)
    mods = [m.reshape(-1, 1, d) for m in (gate1, shift2, scale2, gate2)]

    row_spec = lambda w: pl.BlockSpec((tm, w), lambda i, j: (i, 0))
    mod_spec = pl.BlockSpec((None, 1, d), lambda i, j: (i // per, 0, 0))
    resident = lambda shape: pl.BlockSpec(shape, lambda i, j: (0,) * len(shape), pipeline_mode=pl.Buffered(1))
    seq = lambda i, j: (i * n_phase + j) // n_slab
    sl = lambda i, j: (i * n_phase + j) % n_slab
    assert n_phase % n_slab == 0
    new_spec = pl.BlockSpec((n_phase // n_slab, t, a), lambda i, j: (i, 0, 0))
    win_spec = pl.BlockSpec((None, slab, wb), lambda i, j: (seq(i, j), sl(i, j), 0))
    tab_spec = pl.BlockSpec((t, nk), lambda i, j: (0, 0))
    return pl.pallas_call(
        functools.partial(_tail_sample_kernel, n_slab=n_slab, slab=slab, row_chunk=row_chunk,
                          ff_chunk=ff_chunk),
        out_shape=[jax.ShapeDtypeStruct((rows, d), F32),
                   jax.ShapeDtypeStruct((db, t, a), F32),
                   jax.ShapeDtypeStruct((db, a, wb), F32),
                   jax.ShapeDtypeStruct((db, a, wb), F32)],
        grid=(n_tiles, n_phase),
        in_specs=[row_spec(d), row_spec(a), row_spec(p), mod_spec, mod_spec, mod_spec, mod_spec,
                  pl.BlockSpec((1, d), lambda i, j: (0, 0)),
                  resident((a + p, d)), resident((d, d_ff)), resident((d_ff, d)),
                  pl.BlockSpec(memory_space=pltpu.SMEM), tab_spec, tab_spec,
                  new_spec, new_spec, new_spec,
                  pl.BlockSpec(memory_space=pl.ANY), pl.BlockSpec(memory_space=pl.ANY)],
        out_specs=[row_spec(d), new_spec, win_spec, win_spec],
        scratch_shapes=[pltpu.VMEM((tm, d), BF16), pltpu.VMEM((tm, d), F32),
                        pltpu.VMEM((slab, nk), BF16), pltpu.VMEM((slab, nk), BF16),
                        pltpu.VMEM((n_slab, hps * t, nk), F32),
                        pltpu.VMEM((RING_DEPTH, slab, wb), F32), pltpu.VMEM((RING_DEPTH, slab, wb), F32),
                        pltpu.SemaphoreType.DMA((RING_DEPTH, 2))],
        compiler_params=pltpu.CompilerParams(dimension_semantics=("arbitrary", "arbitrary"),
                                             vmem_limit_bytes=VMEM_LIMIT_BIG),
        name="tail_sample",
    )(x2, attn2, pool2, *mods, g2, w_out_b, w_up_b, w_down_b,
      rel_bias, bkt, mult, q, k_new, v_new, cache_kt, cache_vt)


def _tail_kernel(x_ref, attn_ref, pool_ref, gate1_ref, shift2_ref, scale2_ref, gate2_ref, g2_ref,
                 wout_ref, wup_ref, wdown_ref, y_ref, *, groups, ff_chunk):
    a = attn_ref.shape[1]
    d_ff = wup_ref.shape[1]
    mix = jnp.dot(attn_ref[...].astype(BF16), wout_ref[0:a, :], preferred_element_type=F32)
    mix = mix + jnp.dot(pool_ref[...].astype(BF16), wout_ref[a:, :], preferred_element_type=F32)
    x1 = x_ref[...] + _per_group(gate1_ref[...], mix, groups)
    hb = _modulated_norm(x1, g2_ref[...], shift2_ref[...], scale2_ref[...], groups).astype(BF16)
    f = jnp.zeros(x1.shape, F32)
    for c in range(d_ff // ff_chunk):
        cols = slice(c * ff_chunk, (c + 1) * ff_chunk)
        act = jnp.maximum(jnp.dot(hb, wup_ref[:, cols], preferred_element_type=F32), 0.0)
        f = f + jnp.dot((act * act).astype(BF16), wdown_ref[cols, :], preferred_element_type=F32)
    y_ref[...] = x1 + _per_group(gate2_ref[...], f, groups)


def _layer_tail(x2, attn2, pool2, gate1, shift2, scale2, gate2, g2, w_out_b, w_up_b, w_down_b,
                *, tm, rows_per_mod, ff_chunk=1024):
    rows, d = x2.shape
    a = attn2.shape[1]
    p = pool2.shape[1]
    d_ff = w_up_b.shape[1]
    mods = [gate1, shift2, scale2, gate2]
    if rows_per_mod >= tm:
        groups = 1
        per = rows_per_mod // tm
        mods = [m.reshape(-1, 1, d) for m in mods]
        mod_spec = pl.BlockSpec((None, 1, d), lambda i: (i // per, 0, 0))
    else:
        groups = tm // rows_per_mod
        mod_spec = pl.BlockSpec((groups, d), lambda i: (i, 0))
    const = lambda i: (0, 0)
    row_spec = lambda w: pl.BlockSpec((tm, w), lambda i: (i, 0))
    resident = lambda shape: pl.BlockSpec(shape, const, pipeline_mode=pl.Buffered(1))
    return pl.pallas_call(
        functools.partial(_tail_kernel, groups=groups, ff_chunk=ff_chunk),
        out_shape=jax.ShapeDtypeStruct((rows, d), F32),
        grid=(rows // tm,),
        in_specs=[row_spec(d), row_spec(a), row_spec(p), mod_spec, mod_spec, mod_spec, mod_spec,
                  pl.BlockSpec((1, d), const),
                  resident((a + p, d)), resident((d, d_ff)), resident((d_ff, d))],
        out_specs=row_spec(d),
        compiler_params=pltpu.CompilerParams(dimension_semantics=("arbitrary",),
                                             vmem_limit_bytes=VMEM_LIMIT),
        name="layer_tail",
    )(x2, attn2, pool2, *mods, g2, w_out_b, w_up_b, w_down_b)


def kernel(x_prompt, x_sample, c_prompt, c_sample, cache_k, cache_v, state_pool, w_ada, b_ada, norm1_g, norm2_g, w_in, q_norm_g, k_norm_g, rel_bias, w_pool, pool_scale, w_out, w_up, w_down):
    b, s_len, d = x_prompt.shape
    db, t, _ = x_sample.shape
    depth = w_ada.shape[0]
    assert depth == 1, "single-layer step"
    n_heads = rel_bias.shape[1]
    a = n_heads * HEAD_DIM
    wb = cache_k.shape[2]
    l = 0

    ada = _adaln(jnp.concatenate([c_prompt, c_sample], axis=0), w_ada[l], b_ada[l])
    mods = [ada[:, i * d:(i + 1) * d] for i in range(N_ADA)]
    mp = [m[:b] for m in mods]
    msm = [m[b:] for m in mods]

    w_in_b = w_in[l].astype(BF16)
    w_out_b = w_out[l].astype(BF16)
    w_up_b = w_up[l].astype(BF16)
    w_down_b = w_down[l].astype(BF16)
    w_pool_b = w_pool[l].astype(BF16)
    g1 = norm1_g[l].reshape(1, d)
    g2 = norm2_g[l].reshape(1, d)
    qg = jnp.tile(q_norm_g[l], n_heads).reshape(1, a)
    kg = jnp.tile(k_norm_g[l], n_heads).reshape(1, a)
    ps = pool_scale[l].reshape(1, -1)
    seg_i = jnp.arange(SEG_W) // HEAD_DIM
    seg = jnp.where(seg_i[:, None] == seg_i[None, :], 1.0 / HEAD_DIM, 0.0).astype(BF16)

    xp2 = x_prompt.reshape(b * s_len, d)
    q, k, v, pool, u_last, kt, vt = _mixer_in(xp2, mp[0], mp[1], g1, w_in_b, qg, kg, seg, w_pool_b, ps, None,
                                              tm=1024, rows_per_mod=s_len, pos0=0, feature_major=True)
    attn = _prompt_attention(q.reshape(b, s_len, a), k.reshape(b, s_len, a), v.reshape(b, s_len, a), rel_bias)
    keep = min(max(w for w, _ in DILATED_BRANCHES), s_len)
    win = lambda c: jnp.transpose(c.reshape(b, n_heads, HEAD_DIM, s_len), (0, 3, 1, 2))[None, :, s_len - keep:]
    k_win_prompt = win(kt)
    v_win_prompt = win(vt)
    pool_prompt = u_last[None, :, POOL_PAD - POOL_CTX:]

    xs2 = x_sample.reshape(db * t, d)
    qs, ks, vs, pool_s, us_last = _mixer_in(xs2, msm[0], msm[1], g1, w_in_b, qg, kg, seg, w_pool_b, ps,
                                            state_pool[l], tm=256, rows_per_mod=t, pos0=PAST_LEN)
    to_fm = lambda c: jnp.transpose(c, (0, 2, 3, 1)).reshape(db, a, wb)
    from_fm = lambda c: jnp.transpose(c.reshape(db, n_heads, HEAD_DIM, wb), (0, 3, 1, 2))[None]
    y_prompt, attn_s, k_win, v_win = _tail_and_sample_attention(
        xp2, attn.reshape(b * s_len, a), pool,
        mp[2], mp[3], mp[4], mp[5], g2, w_out_b, w_up_b, w_down_b,
        qs.reshape(db, t, a), ks.reshape(db, t, a), vs.reshape(db, t, a),
        to_fm(cache_k[l]), to_fm(cache_v[l]), rel_bias, tm=512, rows_per_mod=s_len)
    y_prompt = y_prompt.reshape(b, s_len, d)
    y_sample = _layer_tail(xs2, attn_s.reshape(db * t, a), pool_s,
                           msm[2], msm[3], msm[4], msm[5], g2, w_out_b, w_up_b, w_down_b,
                           tm=256, rows_per_mod=t).reshape(db, t, d)
    k_win_sample = from_fm(k_win)
    v_win_sample = from_fm(v_win)
    pool_sample = us_last[None, :, POOL_PAD - POOL_CTX:]

    return (y_prompt, y_sample, k_win_prompt, v_win_prompt, pool_prompt,
            k_win_sample, v_win_sample, pool_sample)
```
